```python
import jax, jax.numpy as jnp
from jax import lax
import numpy as np

D_MODEL = 2048
BATCH = 4
SEQ = 2048
DEPTH = 2
DEC_BATCH = 8
DEC_SEQ = 1
PAST_LEN = 16384
PAGE_SIZE = 128

MIX_WIDTH = D_MODEL
POOL_WIDTH = D_MODEL // 4
POOL_WINDOWS = (2, 4, 8, 16)
POOL_GROUPS = len(POOL_WINDOWS)
POOL_GROUP_DIM = POOL_WIDTH // POOL_GROUPS
POOL_STATE = max(POOL_WINDOWS) - 1
CONV_WIDTH = D_MODEL // 4
CONV_K = 3
ATT_WIDTH = MIX_WIDTH - POOL_WIDTH - CONV_WIDTH
HEAD_DIM = 128
N_HEADS = ATT_WIDTH // HEAD_DIM
D_FF = ((8 * D_MODEL // 3 + 255) // 256) * 256
FFN_K = 3
QBLOCK = 128
EPS = 1e-6
D_IN = POOL_WIDTH + 3 * CONV_WIDTH + 3 * ATT_WIDTH + N_HEADS

kernel_name = "hybrid_pool_conv_fox_step"


def rms_norm(x, g):
    x32 = x.astype(jnp.float32)
    y = x32 * lax.rsqrt(jnp.mean(x32 * x32, axis=-1, keepdims=True) + EPS)
    return (y * g.astype(jnp.float32)).astype(x.dtype)


def causal_dwconv(z, prev, w):
    t = z.shape[1]
    zz = jnp.concatenate([prev.astype(z.dtype), z], axis=1)
    out = w[0] * zz[:, 0:t]
    for j in range(1, w.shape[0]):
        out = out + w[j] * zz[:, j:j + t]
    return out, zz[:, zz.shape[1] - (w.shape[0] - 1):]


def causal_pool_mix(u, prev, pool_w, pool_scale):
    b, t, _ = u.shape
    p = prev.shape[1]
    zz = jnp.concatenate([prev.astype(u.dtype), u], axis=1)
    cs = jnp.pad(jnp.cumsum(zz.astype(jnp.float32), axis=1), ((0, 0), (1, 0), (0, 0)))
    hi = p + jnp.arange(t) + 1
    means = []
    for g, win in enumerate(POOL_WINDOWS):
        lo = jnp.maximum(hi - win, 0)
        sl = slice(g * POOL_GROUP_DIM, (g + 1) * POOL_GROUP_DIM)
        cnt = (hi - lo).astype(jnp.float32)[None, :, None]
        means.append((cs[:, hi, sl] - cs[:, lo, sl]) / cnt)
    pooled = (jnp.concatenate(means, axis=-1) - u.astype(jnp.float32)).astype(u.dtype)
    y = jnp.einsum('btgc,gcd->btgd', pooled.reshape(b, t, POOL_GROUPS, POOL_GROUP_DIM), pool_w)
    y = y.reshape(b, t, POOL_WIDTH) * pool_scale
    return y, zz[:, zz.shape[1] - POOL_STATE:]


def fox_attend(q, k, v, c_q, c_k, q_pos, k_pos):
    s = jnp.einsum('bqhd,bkhd->bhqk', q, k, preferred_element_type=jnp.float32) * (HEAD_DIM ** -0.5)
    s = s + jnp.transpose(c_q, (0, 2, 1))[:, :, :, None] - jnp.transpose(c_k, (0, 2, 1))[:, :, None, :]
    mask = k_pos[None, :] <= q_pos[:, None]
    s = jnp.where(mask[None, None], s, -jnp.inf)
    p = jax.nn.softmax(s, axis=-1)
    return jnp.einsum('bhqk,bkhd->bqhd', p.astype(v.dtype), v)


def fox_prompt(q, k, v, logf):
    b, s, h, d = q.shape
    c = jnp.cumsum(logf.astype(jnp.float32), axis=1)
    nb = s // QBLOCK
    qb = q.reshape(b, nb, QBLOCK, h, d).transpose(1, 0, 2, 3, 4)
    cb = c.reshape(b, nb, QBLOCK, h).transpose(1, 0, 2, 3)
    starts = jnp.arange(nb) * QBLOCK
    k_pos = jnp.arange(s)

    def block(args):
        qi, ci, st = args
        return fox_attend(qi, k, v, ci, c, st + jnp.arange(QBLOCK), k_pos)

    o = lax.map(block, (qb, cb, starts))
    return o.transpose(1, 0, 2, 3, 4).reshape(b, s, h * d)


def fox_sample(q, k_new, v_new, logf_new, k_past, v_past, logf_past):
    b, t, h, d = q.shape
    p = k_past.shape[1]
    k = jnp.concatenate([k_past.astype(k_new.dtype), k_new], axis=1)
    v = jnp.concatenate([v_past.astype(v_new.dtype), v_new], axis=1)
    logf = jnp.concatenate([logf_past.astype(jnp.float32), logf_new.astype(jnp.float32)], axis=1)
    c = jnp.cumsum(logf, axis=1)
    o = fox_attend(q, k, v, c[:, p:], c, p + jnp.arange(t), jnp.arange(p + t))
    return o.reshape(b, t, h * d)


def hybrid_layer(x, prev_pool, prev_conv, prev_ffn, attn_fn, norm1_g, w_in, b_f, pool_w, pool_scale, conv_w,
                 q_norm_g, k_norm_g, out_norm_g, w_o, norm2_g, w_up, ffn_conv_w, ffn_conv_b, w_down):
    b, t, _ = x.shape
    xn = rms_norm(x, norm1_g)
    proj = xn @ w_in
    o1 = POOL_WIDTH
    o2 = o1 + CONV_WIDTH
    o3 = o2 + CONV_WIDTH
    o4 = o3 + CONV_WIDTH
    o5 = o4 + ATT_WIDTH
    o6 = o5 + ATT_WIDTH
    o7 = o6 + ATT_WIDTH
    u_pool, h_b, h_c, h_x, q, k, v, f_logit = jnp.split(proj, [o1, o2, o3, o4, o5, o6, o7], axis=-1)
    y_pool, pool_state = causal_pool_mix(u_pool, prev_pool, pool_w, pool_scale)
    conv_out, conv_state = causal_dwconv(h_c * h_x, prev_conv, conv_w)
    y_conv = h_b * conv_out
    q = rms_norm(q.reshape(b, t, N_HEADS, HEAD_DIM), q_norm_g)
    k = rms_norm(k.reshape(b, t, N_HEADS, HEAD_DIM), k_norm_g)
    v = v.reshape(b, t, N_HEADS, HEAD_DIM)
    logf = jax.nn.log_sigmoid(f_logit.astype(jnp.float32) + b_f.astype(jnp.float32))
    y_att = attn_fn(q, k, v, logf)
    g_pool, g_conv, g_att = jnp.split(out_norm_g, [POOL_WIDTH, POOL_WIDTH + CONV_WIDTH])
    mix = jnp.concatenate([rms_norm(y_pool, g_pool), rms_norm(y_conv, g_conv), rms_norm(y_att, g_att)], axis=-1)
    x = x + mix @ w_o
    xn2 = rms_norm(x, norm2_g)
    a, gate = jnp.split(xn2 @ w_up, 2, axis=-1)
    a_conv, ffn_state = causal_dwconv(a, prev_ffn, ffn_conv_w)
    x = x + (jax.nn.silu(a_conv + ffn_conv_b) * gate) @ w_down
    return x, pool_state, conv_state, ffn_state, k, v, logf


def setup_inputs(seed: int = 0) -> dict:
    key = jax.random.key(seed)
    ks = jax.random.split(key, 32)
    f32 = jnp.float32
    n_pages = PAST_LEN // PAGE_SIZE
    n_used = DEC_BATCH * n_pages
    n_pool = n_used + max(1, n_used // 4)
    nrm = lambda k, shape: jax.random.normal(k, shape, f32)
    perm = jax.random.permutation(ks[0], n_pool)[:n_used]
    page_table = perm.reshape(DEC_BATCH, n_pages).astype(jnp.int32)
    return {
        "x_prompt": nrm(ks[1], (BATCH, SEQ, D_MODEL)),
        "x_sample": nrm(ks[2], (DEC_BATCH, DEC_SEQ, D_MODEL)),
        "cache_k": nrm(ks[3], (DEPTH, n_pool, PAGE_SIZE, N_HEADS, HEAD_DIM)),
        "cache_v": nrm(ks[4], (DEPTH, n_pool, PAGE_SIZE, N_HEADS, HEAD_DIM)),
        "cache_logf": jax.nn.log_sigmoid(2.5 + nrm(ks[5], (DEPTH, n_pool, PAGE_SIZE, N_HEADS))),
        "state_pool": nrm(ks[6], (DEPTH, DEC_BATCH, POOL_STATE, POOL_WIDTH)),
        "state_conv": nrm(ks[7], (DEPTH, DEC_BATCH, CONV_K - 1, CONV_WIDTH)),
        "state_ffn": nrm(ks[8], (DEPTH, DEC_BATCH, FFN_K - 1, D_FF)),
        "page_table": page_table,
        "norm1_g": 1.0 + 0.05 * nrm(ks[9], (DEPTH, D_MODEL)),
        "w_in": nrm(ks[10], (DEPTH, D_MODEL, D_IN)) * D_MODEL ** -0.5,
        "b_f": jax.random.uniform(ks[11], (DEPTH, N_HEADS), f32, 1.0, 4.0),
        "pool_w": nrm(ks[12], (DEPTH, POOL_GROUPS, POOL_GROUP_DIM, POOL_GROUP_DIM)) * POOL_GROUP_DIM ** -0.5,
        "pool_scale": 1.0 + 0.05 * nrm(ks[13], (DEPTH, POOL_WIDTH)),
        "conv_w": nrm(ks[14], (DEPTH, CONV_K, CONV_WIDTH)) * CONV_K ** -0.5,
        "q_norm_g": 1.0 + 0.05 * nrm(ks[15], (DEPTH, HEAD_DIM)),
        "k_norm_g": 1.0 + 0.05 * nrm(ks[16], (DEPTH, HEAD_DIM)),
        "out_norm_g": 1.0 + 0.05 * nrm(ks[17], (DEPTH, MIX_WIDTH)),
        "w_o": nrm(ks[18], (DEPTH, MIX_WIDTH, D_MODEL)) * MIX_WIDTH ** -0.5,
        "norm2_g": 1.0 + 0.05 * nrm(ks[19], (DEPTH, D_MODEL)),
        "w_up": nrm(ks[20], (DEPTH, D_MODEL, 2 * D_FF)) * D_MODEL ** -0.5,
        "ffn_conv_w": nrm(ks[21], (DEPTH, FFN_K, D_FF)) * FFN_K ** -0.5,
        "ffn_conv_b": 0.02 * nrm(ks[22], (DEPTH, D_FF)),
        "w_down": nrm(ks[23], (DEPTH, D_FF, D_MODEL)) * D_FF ** -0.5,
    }


def reference(x_prompt, x_sample, cache_k, cache_v, cache_logf, state_pool, state_conv, state_ffn, page_table,
              norm1_g, w_in, b_f, pool_w, pool_scale, conv_w, q_norm_g, k_norm_g, out_norm_g, w_o, norm2_g,
              w_up, ffn_conv_w, ffn_conv_b, w_down):
    b, _, _ = x_prompt.shape
    db = x_sample.shape[0]
    past = page_table.shape[1] * PAGE_SIZE
    dt = x_prompt.dtype
    yp, ys = x_prompt, x_sample
    kp_l, vp_l, fp_l, pp_l, cp_l, ffp_l = [], [], [], [], [], []
    ks_l, vs_l, fs_l, ps_l, cs_l, ffs_l = [], [], [], [], [], []
    for l in range(DEPTH):
        w = (norm1_g[l], w_in[l], b_f[l], pool_w[l], pool_scale[l], conv_w[l], q_norm_g[l], k_norm_g[l],
             out_norm_g[l], w_o[l], norm2_g[l], w_up[l], ffn_conv_w[l], ffn_conv_b[l], w_down[l])
        yp, ps, cs, ffs, kk, vv, ff = hybrid_layer(
            yp, jnp.zeros((b, 0, POOL_WIDTH), dt), jnp.zeros((b, CONV_K - 1, CONV_WIDTH), dt),
            jnp.zeros((b, FFN_K - 1, D_FF), dt), fox_prompt, *w)
        kp_l.append(kk); vp_l.append(vv); fp_l.append(ff); pp_l.append(ps); cp_l.append(cs); ffp_l.append(ffs)
        k_past = cache_k[l, page_table].reshape(db, past, N_HEADS, HEAD_DIM)
        v_past = cache_v[l, page_table].reshape(db, past, N_HEADS, HEAD_DIM)
        f_past = cache_logf[l, page_table].reshape(db, past, N_HEADS)
        attn_s = lambda q, k, v, f, kp=k_past, vp=v_past, fpst=f_past: fox_sample(q, k, v, f, kp, vp, fpst)
        ys, ps, cs, ffs, kk, vv, ff = hybrid_layer(ys, state_pool[l], state_conv[l], state_ffn[l], attn_s, *w)
        ks_l.append(kk); vs_l.append(vv); fs_l.append(ff); ps_l.append(ps); cs_l.append(cs); ffs_l.append(ffs)
    return (yp, ys,
            jnp.stack(kp_l), jnp.stack(vp_l), jnp.stack(fp_l), jnp.stack(pp_l), jnp.stack(cp_l), jnp.stack(ffp_l),
            jnp.stack(ks_l), jnp.stack(vs_l), jnp.stack(fs_l), jnp.stack(ps_l), jnp.stack(cs_l), jnp.stack(ffs_l))
```

```python
import functools

import jax
import jax.numpy as jnp
from jax import lax
from jax.experimental import pallas as pl
from jax.experimental.pallas import tpu as pltpu

F32 = jnp.float32
BF16 = jnp.bfloat16

D_MODEL = 2048
PAGE_SIZE = 128
POOL_WIDTH = 512
POOL_WINDOWS = (2, 4, 8, 16)
POOL_GROUP_DIM = 128
POOL_STATE = 15
CONV_WIDTH = 512
CONV_K = 3
ATT_WIDTH = 1024
HEAD_DIM = 128
N_HEADS = 8
D_FF = 5632
FFN_K = 3
EPS = 1e-6
D_PROJ = POOL_WIDTH + 3 * CONV_WIDTH + 3 * ATT_WIDTH
SCALE = HEAD_DIM ** -0.5

LANES = 128
SUBLANES = 8
VMEM_LIMIT = 56 * 1024 * 1024

PROJ_TN = 512
PAGES_PER_STEP = 8


def _cparams(sem):
    return pltpu.CompilerParams(dimension_semantics=sem, vmem_limit_bytes=VMEM_LIMIT)


def _rms(x, g):
    ms = jnp.mean(x * x, axis=-1, keepdims=True)
    return x * lax.rsqrt(ms + EPS) * g


def _log_sigmoid(z):
    return jnp.minimum(z, 0.0) - jnp.log1p(jnp.exp(-jnp.abs(z)))


def _split3(x):
    p0 = x.astype(BF16)
    r = x - p0.astype(F32)
    p1 = r.astype(BF16)
    p2 = (r - p1.astype(F32)).astype(BF16)
    return p0, p1, p2


def _dot01(a_pieces, b):
    out = jnp.dot(a_pieces[0], b, preferred_element_type=F32)
    for p in a_pieces[1:]:
        out = out + jnp.dot(p, b, preferred_element_type=F32)
    return out


def _inproj_kernel(x_ref, g1_ref, w_ref, wf_ref, bf_ref, qg_ref, kg_ref,
                   proj_ref, q_ref, k32_ref, kb_ref, v32_ref, vb_ref, logf_ref, logfp_ref, xn_ref, *, tm):
    n = pl.program_id(1)
    rc = min(tm, 256)

    @pl.when(n == 0)
    def _():
        def body(i, c):
            r = pl.ds(pl.multiple_of(i * rc, rc), rc)
            xn_ref[r, :] = _rms(x_ref[r, :], g1_ref[...]).astype(BF16)
            return c
        lax.fori_loop(0, tm // rc, body, 0)
        fl = jnp.dot(xn_ref[...], wf_ref[...], preferred_element_type=F32)
        lf = _log_sigmoid(fl + bf_ref[...])
        lane = lax.broadcasted_iota(jnp.int32, lf.shape, 1)
        lf = jnp.where(lane < N_HEADS, lf, 0.0)
        logfp_ref[...] = lf
        logf_ref[...] = lf[:, :N_HEADS]

    y = jnp.dot(xn_ref[...], w_ref[...], preferred_element_type=F32)

    def head_norm(g):
        parts = []
        for h in range(PROJ_TN // HEAD_DIM):
            parts.append(_rms(y[:, h * HEAD_DIM:(h + 1) * HEAD_DIM], g))
        return jnp.concatenate(parts, axis=1)

    @pl.when(n < 4)
    def _():
        proj_ref[...] = y

    @pl.when((n >= 4) & (n < 6))
    def _():
        q_ref[...] = head_norm(qg_ref[...]).astype(BF16)

    @pl.when((n >= 6) & (n < 8))
    def _():
        kn = head_norm(kg_ref[...])
        k32_ref[...] = kn
        kb_ref[...] = kn.astype(BF16)

    @pl.when(n >= 8)
    def _():
        v32_ref[...] = y
        vb_ref[...] = y.astype(BF16)


def _inproj(x, g1, w_in_b, wf_b, bf_pad, qg, kg, tm):
    m = x.shape[0]
    nt = D_PROJ // PROJ_TN
    col = lambda lo: (lambda i, n: (i, jnp.clip(n - lo, 0, 1)))
    out_shape = (
        jax.ShapeDtypeStruct((m, 4 * PROJ_TN), F32),
        jax.ShapeDtypeStruct((m, ATT_WIDTH), BF16),
        jax.ShapeDtypeStruct((m, ATT_WIDTH), F32),
        jax.ShapeDtypeStruct((m, ATT_WIDTH), BF16),
        jax.ShapeDtypeStruct((m, ATT_WIDTH), F32),
        jax.ShapeDtypeStruct((m, ATT_WIDTH), BF16),
        jax.ShapeDtypeStruct((m, N_HEADS), F32),
        jax.ShapeDtypeStruct((m, LANES), F32),
    )
    out_specs = (
        pl.BlockSpec((tm, PROJ_TN), lambda i, n: (i, jnp.minimum(n, 3))),
        pl.BlockSpec((tm, PROJ_TN), col(4)),
        pl.BlockSpec((tm, PROJ_TN), col(6)),
        pl.BlockSpec((tm, PROJ_TN), col(6)),
        pl.BlockSpec((tm, PROJ_TN), col(8)),
        pl.BlockSpec((tm, PROJ_TN), col(8)),
        pl.BlockSpec((tm, N_HEADS), lambda i, n: (i, 0)),
        pl.BlockSpec((tm, LANES), lambda i, n: (i, 0)),
    )
    in_specs = [
        pl.BlockSpec((tm, D_MODEL), lambda i, n: (i, 0)),
        pl.BlockSpec((1, D_MODEL), lambda i, n: (0, 0)),
        pl.BlockSpec((D_MODEL, PROJ_TN), lambda i, n: (0, n)),
        pl.BlockSpec((D_MODEL, LANES), lambda i, n: (0, 0)),
        pl.BlockSpec((1, LANES), lambda i, n: (0, 0)),
        pl.BlockSpec((1, HEAD_DIM), lambda i, n: (0, 0)),
        pl.BlockSpec((1, HEAD_DIM), lambda i, n: (0, 0)),
    ]
    return pl.pallas_call(
        functools.partial(_inproj_kernel, tm=tm),
        grid=(m // tm, nt),
        in_specs=in_specs, out_specs=out_specs, out_shape=out_shape,
        scratch_shapes=[pltpu.VMEM((tm, D_MODEL), BF16)],
        compiler_params=_cparams(("arbitrary", "arbitrary")),
        name="inproj",
    )(x, g1, w_in_b, wf_b, bf_pad, qg, kg)


CUM_BLK = 256


def _cumsum_kernel(lf_ref, ccol_ref, crow_ref, *, s):
    r = lax.broadcasted_iota(jnp.int32, (CUM_BLK, CUM_BLK), 0)
    c = lax.broadcasted_iota(jnp.int32, (CUM_BLK, CUM_BLK), 1)
    tri = (c <= r).astype(BF16)
    carry = jnp.zeros((1, LANES), F32)
    for b in range(s // CUM_BLK):
        rows = slice(b * CUM_BLK, (b + 1) * CUM_BLK)
        lf = lf_ref[0, rows, :]
        pieces = _split3(lf)
        cs = jnp.dot(tri, pieces[0], preferred_element_type=F32)
        cs = cs + jnp.dot(tri, pieces[1], preferred_element_type=F32)
        cs = cs + jnp.dot(tri, pieces[2], preferred_element_type=F32)
        cs = cs + carry
        carry = cs[CUM_BLK - 1:CUM_BLK, :]
        ccol_ref[0, rows, :] = cs[:, :N_HEADS]
        crow_ref[0, :, rows] = cs.T[:N_HEADS, :]


def _cumsum(logf_pad, nseq, s):
    lf3 = logf_pad.reshape(nseq, s, LANES)
    return pl.pallas_call(
        functools.partial(_cumsum_kernel, s=s),
        grid=(nseq,),
        in_specs=[pl.BlockSpec((1, s, LANES), lambda b: (b, 0, 0))],
        out_specs=(pl.BlockSpec((1, s, N_HEADS), lambda b: (b, 0, 0)),
                   pl.BlockSpec((1, N_HEADS, s), lambda b: (b, 0, 0))),
        out_shape=(jax.ShapeDtypeStruct((nseq, s, N_HEADS), F32),
                   jax.ShapeDtypeStruct((nseq, N_HEADS, s), F32)),
        compiler_params=_cparams(("arbitrary",)),
        name="logf_cumsum",
    )(lf3)


POOL_HALO = 16
CONV_HALO = 8


def _mix_kernel(u_ref, hb_ref, hc_ref, hx_ref, pw_ref, ps_ref, cw_ref, gp_ref, gc_ref,
                mix_ref, pstate_ref, cstate_ref, ubuf, zbuf, *, ts):
    si = pl.program_id(1)

    @pl.when(si == 0)
    def _():
        ubuf[0:POOL_HALO, :] = jnp.zeros((POOL_HALO, POOL_WIDTH), F32)
        zbuf[0:CONV_HALO, :] = jnp.zeros((CONV_HALO, CONV_WIDTH), F32)

    @pl.when(si > 0)
    def _():
        ubuf[0:POOL_HALO, :] = ubuf[ts:ts + POOL_HALO, :]
        zbuf[0:CONV_HALO, :] = zbuf[ts:ts + CONV_HALO, :]

    u = u_ref[...]
    ubuf[POOL_HALO:POOL_HALO + ts, :] = u
    z = hc_ref[...] * hx_ref[...]
    zbuf[CONV_HALO:CONV_HALO + ts, :] = z

    pos = si * ts + lax.broadcasted_iota(jnp.int32, (ts, 1), 0)
    ys = []
    for g, win in enumerate(POOL_WINDOWS):
        cols = slice(g * POOL_GROUP_DIM, (g + 1) * POOL_GROUP_DIM)
        ug = u[:, cols]
        acc = ug
        for j in range(1, win):
            acc = acc + ubuf[POOL_HALO - j:POOL_HALO - j + ts, cols]
        cnt = jnp.minimum(pos + 1, win).astype(F32)
        pooled = (acc / cnt - ug).astype(BF16)
        ys.append(jnp.dot(pooled, pw_ref[g], preferred_element_type=F32))
    y_pool = jnp.concatenate(ys, axis=1) * ps_ref[...]
    mix_ref[:, 0:POOL_WIDTH] = _rms(y_pool, gp_ref[...]).astype(BF16)

    conv = (cw_ref[0:1, :] * zbuf[CONV_HALO - 2:CONV_HALO - 2 + ts, :]
            + cw_ref[1:2, :] * zbuf[CONV_HALO - 1:CONV_HALO - 1 + ts, :]
            + cw_ref[2:3, :] * z)
    y_conv = hb_ref[...] * conv
    mix_ref[:, POOL_WIDTH:POOL_WIDTH + CONV_WIDTH] = _rms(y_conv, gc_ref[...]).astype(BF16)

    pstate_ref[0] = ubuf[POOL_HALO + ts - POOL_STATE:POOL_HALO + ts, :]
    cstate_ref[0] = zbuf[CONV_HALO + ts - (CONV_K - 1):CONV_HALO + ts, :]


def _mixers(proj, pool_w_b, pool_scale, conv_w, g_pool, g_conv, nseq, s, ts=512):
    m = proj.shape[0]
    nst = s // ts
    pcol = lambda j: pl.BlockSpec((ts, PROJ_TN), lambda b, i: (b * nst + i, j))
    const = lambda shape: pl.BlockSpec(shape, lambda b, i: (0,) * len(shape))
    return pl.pallas_call(
        functools.partial(_mix_kernel, ts=ts),
        grid=(nseq, nst),
        in_specs=[pcol(0), pcol(1), pcol(2), pcol(3),
                  const((len(POOL_WINDOWS), POOL_GROUP_DIM, POOL_GROUP_DIM)),
                  const((1, POOL_WIDTH)), const((CONV_K, CONV_WIDTH)),
                  const((1, POOL_WIDTH)), const((1, CONV_WIDTH))],
        out_specs=(pl.BlockSpec((ts, POOL_WIDTH + CONV_WIDTH), lambda b, i: (b * nst + i, 0)),
                   pl.BlockSpec((1, POOL_STATE, POOL_WIDTH), lambda b, i: (b, 0, 0)),
                   pl.BlockSpec((1, CONV_K - 1, CONV_WIDTH), lambda b, i: (b, 0, 0))),
        out_shape=(jax.ShapeDtypeStruct((m, POOL_WIDTH + CONV_WIDTH), BF16),
                   jax.ShapeDtypeStruct((nseq, POOL_STATE, POOL_WIDTH), F32),
                   jax.ShapeDtypeStruct((nseq, CONV_K - 1, CONV_WIDTH), F32)),
        scratch_shapes=[pltpu.VMEM((POOL_HALO + ts, POOL_WIDTH), F32),
                        pltpu.VMEM((CONV_HALO + ts, CONV_WIDTH), F32)],
        compiler_params=_cparams(("arbitrary", "arbitrary")),
        name="prompt_mixers",
    )(proj, proj, proj, proj, pool_w_b, pool_scale, conv_w, g_pool, g_conv)


def _fox_kernel(q_ref, k_ref, v_ref, cq_ref, ck_ref, g_ref, out_ref, y_scr, *, tq):
    i = pl.program_id(1)
    row = lax.broadcasted_iota(jnp.int32, (tq, tq), 0)
    col = lax.broadcasted_iota(jnp.int32, (tq, tq), 1)
    causal = col <= row

    for h in range(N_HEADS):
        cs = slice(h * HEAD_DIM, (h + 1) * HEAD_DIM)
        q = q_ref[:, cs]
        cq = cq_ref[0, :, h:h + 1]

        def step(j, carry, masked):
            m, l, acc = carry
            ks = pl.ds(pl.multiple_of(j * tq, tq), tq)
            k = k_ref[ks, cs]
            v = v_ref[ks, cs]
            s = lax.dot_general(q, k, (((1,), (1,)), ((), ())), preferred_element_type=F32)
            s = s * SCALE + cq - ck_ref[0, h:h + 1, ks]
            if masked:
                s = jnp.where(causal, s, -jnp.inf)
            m_new = jnp.maximum(m, jnp.max(s, axis=1, keepdims=True))
            alpha = jnp.exp(m - m_new)
            p = jnp.exp(s - m_new)
            l = alpha * l + jnp.sum(p, axis=1, keepdims=True)
            acc = alpha * acc + jnp.dot(p.astype(BF16), v, preferred_element_type=F32)
            return m_new, l, acc

        init = (jnp.full((tq, 1), -jnp.inf, F32), jnp.zeros((tq, 1), F32), jnp.zeros((tq, HEAD_DIM), F32))
        carry = lax.fori_loop(0, i, functools.partial(step, masked=False), init)
        _, l, acc = step(i, carry, True)
        y_scr[:, cs] = acc / l

    out_ref[...] = _rms(y_scr[...], g_ref[...]).astype(BF16)


def _fox_prompt(qb, kb, vb, ccol, crow, g_att, nseq, s, tq=256):
    m = qb.shape[0]
    nq = s // tq
    return pl.pallas_call(
        functools.partial(_fox_kernel, tq=tq),
        grid=(nseq, nq),
        in_specs=[pl.BlockSpec((tq, ATT_WIDTH), lambda b, i: (b * nq + i, 0)),
                  pl.BlockSpec((s, ATT_WIDTH), lambda b, i: (b, 0)),
                  pl.BlockSpec((s, ATT_WIDTH), lambda b, i: (b, 0)),
                  pl.BlockSpec((1, tq, N_HEADS), lambda b, i: (b, i, 0)),
                  pl.BlockSpec((1, N_HEADS, s), lambda b, i: (b, 0, 0)),
                  pl.BlockSpec((1, ATT_WIDTH), lambda b, i: (0, 0))],
        out_specs=pl.BlockSpec((tq, ATT_WIDTH), lambda b, i: (b * nq + i, 0)),
        out_shape=jax.ShapeDtypeStruct((m, ATT_WIDTH), BF16),
        scratch_shapes=[pltpu.VMEM((tq, ATT_WIDTH), F32)],
        compiler_params=_cparams(("arbitrary", "arbitrary")),
        name="fox_prompt",
    )(qb, kb, vb, ccol, crow, g_att)


def _outproj_kernel(x_ref, ma_ref, mb_ref, wa_ref, wb_ref, o_ref):
    acc = jnp.dot(ma_ref[...], wa_ref[...], preferred_element_type=F32)
    acc = acc + jnp.dot(mb_ref[...], wb_ref[...], preferred_element_type=F32)
    o_ref[...] = x_ref[...] + acc


def _outproj(x, mix_a, mix_b, w_o_b, tm, tn=512):
    m = x.shape[0]
    half = D_MODEL // 2
    return pl.pallas_call(
        _outproj_kernel,
        grid=(m // tm, D_MODEL // tn),
        in_specs=[pl.BlockSpec((tm, tn), lambda i, n: (i, n)),
                  pl.BlockSpec((tm, half), lambda i, n: (i, 0)),
                  pl.BlockSpec((tm, half), lambda i, n: (i, 0)),
                  pl.BlockSpec((half, tn), lambda i, n: (0, n)),
                  pl.BlockSpec((half, tn), lambda i, n: (1, n))],
        out_specs=pl.BlockSpec((tm, tn), lambda i, n: (i, n)),
        out_shape=jax.ShapeDtypeStruct((m, D_MODEL), F32),
        compiler_params=_cparams(("arbitrary", "arbitrary")),
        name="outproj",
    )(x, mix_a, mix_b, w_o_b, w_o_b)


FFN_HALO = 8


def _ffn_kernel(x_ref, g2_ref, wa_ref, wg_ref, cw_ref, cb_ref, wd_ref, prev_ref,
                o_ref, state_ref, xn_ref, abuf, carry_ref, *, tm, tiles_per_seq, per_row_state):
    mi = pl.program_id(0)
    f = pl.program_id(1)
    rc = min(tm, 256)

    @pl.when(f == 0)
    def _():
        def body(i, c):
            r = pl.ds(pl.multiple_of(i * rc, rc), rc)
            x = x_ref[r, :]
            xn_ref[r, :] = _rms(x, g2_ref[...]).astype(BF16)
            o_ref[r, :] = x
            return c
        lax.fori_loop(0, tm // rc, body, 0)

    xn = xn_ref[...]
    a = jnp.dot(xn, wa_ref[...], preferred_element_type=F32)
    gate = jnp.dot(xn, wg_ref[...], preferred_element_type=F32)

    if per_row_state:
        p0 = prev_ref[0]
        p1 = prev_ref[1]
        ac = cw_ref[0:1, :] * p0 + cw_ref[1:2, :] * p1 + cw_ref[2:3, :] * a
        state_ref[0] = p1
        state_ref[1] = a
    else:
        first = (mi % tiles_per_seq) == 0

        @pl.when(first)
        def _():
            abuf[0:FFN_HALO, :] = jnp.zeros((FFN_HALO, a.shape[1]), F32)

        @pl.when(jnp.logical_not(first))
        def _():
            abuf[0:FFN_HALO, :] = carry_ref[f]

        abuf[FFN_HALO:FFN_HALO + tm, :] = a
        carry_ref[f] = a[tm - FFN_HALO:tm, :]
        ac = (cw_ref[0:1, :] * abuf[FFN_HALO - 2:FFN_HALO - 2 + tm, :]
              + cw_ref[1:2, :] * abuf[FFN_HALO - 1:FFN_HALO - 1 + tm, :]
              + cw_ref[2:3, :] * a)
        state_ref[0] = a[tm - (FFN_K - 1):tm, :]
    ac = ac + cb_ref[...]
    act = (ac * jax.nn.sigmoid(ac) * gate).astype(BF16)
    o_ref[...] += jnp.dot(act, wd_ref[...], preferred_element_type=F32)


def _ffn(x, g2, w_up_b, cw, cb, w_down_b, prev, tm, seq_len, per_row_state, tf=512):
    m = x.shape[0]
    nf = D_FF // tf
    if per_row_state:
        tiles_per_seq = 1
        prev_spec = pl.BlockSpec((FFN_K - 1, tm, tf), lambda i, f: (0, i, f))
        state_spec = pl.BlockSpec((FFN_K - 1, tm, tf), lambda i, f: (0, i, f))
        state_shape = jax.ShapeDtypeStruct((FFN_K - 1, m, D_FF), F32)
    else:
        tiles_per_seq = seq_len // tm
        prev_spec = pl.BlockSpec((1, 1, tf), lambda i, f: (0, 0, f))
        state_spec = pl.BlockSpec((1, FFN_K - 1, tf), lambda i, f: (i, 0, f))
        state_shape = jax.ShapeDtypeStruct((m // tm, FFN_K - 1, D_FF), F32)
    x_out, state = pl.pallas_call(
        functools.partial(_ffn_kernel, tm=tm, tiles_per_seq=tiles_per_seq, per_row_state=per_row_state),
        grid=(m // tm, nf),
        in_specs=[pl.BlockSpec((tm, D_MODEL), lambda i, f: (i, 0)),
                  pl.BlockSpec((1, D_MODEL), lambda i, f: (0, 0)),
                  pl.BlockSpec((D_MODEL, tf), lambda i, f: (0, f)),
                  pl.BlockSpec((D_MODEL, tf), lambda i, f: (0, nf + f)),
                  pl.BlockSpec((FFN_K, tf), lambda i, f: (0, f)),
                  pl.BlockSpec((1, tf), lambda i, f: (0, f)),
                  pl.BlockSpec((tf, D_MODEL), lambda i, f: (f, 0)),
                  prev_spec],
        out_specs=(pl.BlockSpec((tm, D_MODEL), lambda i, f: (i, 0)), state_spec),
        out_shape=(jax.ShapeDtypeStruct((m, D_MODEL), F32), state_shape),
        scratch_shapes=[pltpu.VMEM((tm, D_MODEL), BF16),
                        pltpu.VMEM((FFN_HALO + tm, tf), F32),
                        pltpu.VMEM((nf, FFN_HALO, tf), F32)],
        compiler_params=_cparams(("arbitrary", "arbitrary")),
        name="ffn",
    )(x, g2, w_up_b, w_up_b, cw, cb, w_down_b, prev)
    if not per_row_state:
        state = state[tiles_per_seq - 1::tiles_per_seq]
    return x_out, state


PAGE_ROW = PAGE_SIZE * N_HEADS


def _bias_kernel(pt_ref, lf_hbm, lfnew_ref, bias_ref, lfbuf, sem, *, n_pages):
    b = pl.program_id(0)

    def copy(p):
        page = pt_ref[b, p]
        return pltpu.make_async_copy(lf_hbm.at[pl.ds(page, 1)], lfbuf.at[pl.ds(p, 1)], sem)

    def start(p, c):
        copy(p).start()
        return c

    def wait(p, c):
        copy(p).wait()
        return c

    lax.fori_loop(0, n_pages, start, 0)
    lax.fori_loop(0, n_pages, wait, 0)

    src = lax.broadcasted_iota(jnp.int32, (PAGE_ROW, PAGE_ROW), 0)
    dst = lax.broadcasted_iota(jnp.int32, (PAGE_ROW, PAGE_ROW), 1)
    same_head = (src & (N_HEADS - 1)) == (dst & (N_HEADS - 1))
    later = (src >> 3) > (dst >> 3)
    m_suffix = (same_head & later).astype(BF16)
    m_total = same_head.astype(BF16)
    pr = lax.broadcasted_iota(jnp.int32, (n_pages, n_pages), 0)
    pc = lax.broadcasted_iota(jnp.int32, (n_pages, n_pages), 1)
    later_page = (pc > pr).astype(BF16)

    pieces = _split3(lfbuf[...])
    within = _dot01(pieces, m_suffix)
    total = _dot01(pieces, m_total)
    t0, t1, t2 = _split3(total)
    after = jnp.dot(later_page, t0, preferred_element_type=F32)
    after = after + jnp.dot(later_page, t1, preferred_element_type=F32)
    after = after + jnp.dot(later_page, t2, preferred_element_type=F32)
    bias_ref[0] = within + after + lfnew_ref[0]


def _sample_bias(page_table, logf_flat, lfnew_rep):
    db, n_pages = page_table.shape
    grid_spec = pltpu.PrefetchScalarGridSpec(
        num_scalar_prefetch=1,
        grid=(db,),
        in_specs=[pl.BlockSpec(memory_space=pl.ANY),
                  pl.BlockSpec((1, 1, PAGE_ROW), lambda b, pt: (b, 0, 0))],
        out_specs=pl.BlockSpec((1, n_pages, PAGE_ROW), lambda b, pt: (b, 0, 0)),
        scratch_shapes=[pltpu.VMEM((n_pages, PAGE_ROW), F32), pltpu.SemaphoreType.DMA(())],
    )
    return pl.pallas_call(
        functools.partial(_bias_kernel, n_pages=n_pages),
        grid_spec=grid_spec,
        out_shape=jax.ShapeDtypeStruct((db, n_pages, PAGE_ROW), F32),
        compiler_params=_cparams(("arbitrary",)),
        name="sample_bias",
    )(page_table, logf_flat, lfnew_rep)


def _head_allreduce(x, op, reduce_op):
    x = jnp.broadcast_to(reduce_op(x, axis=0, keepdims=True), (SUBLANES, LANES))
    for sh in (8, 16, 32, 64):
        x = op(x, pltpu.roll(x, sh, axis=1))
    return x


def _paged_kernel(pt_ref, q_ref, bias_ref, knew_ref, vnew_ref, *refs, n_steps):
    k_refs = refs[:PAGES_PER_STEP]
    v_refs = refs[PAGES_PER_STEP:2 * PAGES_PER_STEP]
    o_ref = refs[2 * PAGES_PER_STEP]
    m_scr, l_scr, acc_scr, s_scr = refs[2 * PAGES_PER_STEP + 1:]
    i = pl.program_id(1)

    @pl.when(i == 0)
    def _():
        m_scr[...] = jnp.full((SUBLANES, LANES), -jnp.inf, F32)
        l_scr[...] = jnp.zeros((SUBLANES, LANES), F32)
        acc_scr[...] = jnp.zeros((N_HEADS, HEAD_DIM), F32)

    q = q_ref[0]
    sub = lax.broadcasted_iota(jnp.int32, (N_HEADS, PAGE_ROW), 0)
    lane = lax.broadcasted_iota(jnp.int32, (N_HEADS, PAGE_ROW), 1)
    own_head = sub == (lane & (N_HEADS - 1))

    for j in range(PAGES_PER_STEP):
        k2 = k_refs[j][...].reshape(PAGE_ROW, HEAD_DIM).astype(BF16)
        st = lax.dot_general(q, k2, (((1,), (1,)), ((), ())), preferred_element_type=F32)
        s_scr[j:j + 1, :] = jnp.sum(jnp.where(own_head, st, 0.0), axis=0, keepdims=True)

    s = s_scr[...] * SCALE + bias_ref[0]
    chunks = [s[:, c * LANES:(c + 1) * LANES] for c in range(PAGE_ROW // LANES)]
    mx = chunks[0]
    for c in chunks[1:]:
        mx = jnp.maximum(mx, c)
    m_prev = m_scr[...]
    m_new = jnp.maximum(m_prev, _head_allreduce(mx, jnp.maximum, jnp.max))
    alpha = jnp.exp(m_prev - m_new)
    p = jnp.exp(s - jnp.concatenate([m_new] * (PAGE_ROW // LANES), axis=1))
    ps = p[:, 0:LANES]
    for c in range(1, PAGE_ROW // LANES):
        ps = ps + p[:, c * LANES:(c + 1) * LANES]
    l_scr[...] = alpha * l_scr[...] + _head_allreduce(ps, jnp.add, jnp.sum)
    m_scr[...] = m_new

    o = jnp.zeros((N_HEADS, HEAD_DIM), F32)
    for j in range(PAGES_PER_STEP):
        pj = jnp.where(own_head, jnp.broadcast_to(p[j:j + 1, :], (N_HEADS, PAGE_ROW)), 0.0).astype(BF16)
        v2 = v_refs[j][...].reshape(PAGE_ROW, HEAD_DIM).astype(BF16)
        o = o + jnp.dot(pj, v2, preferred_element_type=F32)

    sub8 = lax.broadcasted_iota(jnp.int32, (SUBLANES, LANES), 0)
    lane8 = lax.broadcasted_iota(jnp.int32, (SUBLANES, LANES), 1)
    diag = sub8 == lane8

    def to_col(x):
        return jnp.sum(jnp.where(diag, x, 0.0), axis=1, keepdims=True)

    acc = to_col(alpha) * acc_scr[...] + o
    acc_scr[...] = acc

    @pl.when(i == n_steps - 1)
    def _():
        m_col = to_col(m_new)
        l_col = to_col(l_scr[...])
        qf = q.astype(F32)
        kn = knew_ref[0].astype(BF16).astype(F32)
        s_new = jnp.sum(qf * kn, axis=1, keepdims=True) * SCALE
        m_f = jnp.maximum(m_col, s_new)
        a_old = jnp.exp(m_col - m_f)
        p_new = jnp.exp(s_new - m_f)
        o_ref[0] = (acc * a_old + p_new * vnew_ref[0]) / (l_col * a_old + p_new)


def _paged_attention(layer, page_table, q3, bias, k_new3, v_new3, cache_k, cache_v):
    db, n_pages = page_table.shape
    n_steps = n_pages // PAGES_PER_STEP

    def page_spec(j):
        return pl.BlockSpec((None, None, PAGE_SIZE, N_HEADS, HEAD_DIM),
                            lambda b, i, pt: (layer, pt[b, i * PAGES_PER_STEP + j], 0, 0, 0))

    row = pl.BlockSpec((1, N_HEADS, HEAD_DIM), lambda b, i, pt: (b, 0, 0))
    grid_spec = pltpu.PrefetchScalarGridSpec(
        num_scalar_prefetch=1,
        grid=(db, n_steps),
        in_specs=[row,
                  pl.BlockSpec((1, PAGES_PER_STEP, PAGE_ROW), lambda b, i, pt: (b, i, 0)),
                  row, row]
                 + [page_spec(j) for j in range(PAGES_PER_STEP)]
                 + [page_spec(j) for j in range(PAGES_PER_STEP)],
        out_specs=row,
        scratch_shapes=[pltpu.VMEM((SUBLANES, LANES), F32), pltpu.VMEM((SUBLANES, LANES), F32),
                        pltpu.VMEM((N_HEADS, HEAD_DIM), F32), pltpu.VMEM((PAGES_PER_STEP, PAGE_ROW), F32)],
    )
    return pl.pallas_call(
        functools.partial(_paged_kernel, n_steps=n_steps),
        grid_spec=grid_spec,
        out_shape=jax.ShapeDtypeStruct((db, N_HEADS, HEAD_DIM), F32),
        compiler_params=_cparams(("arbitrary", "arbitrary")),
        name="paged_attention",
    )(page_table, q3, bias, k_new3, v_new3, *([cache_k] * PAGES_PER_STEP), *([cache_v] * PAGES_PER_STEP))


def _sample_mix_kernel(proj_ref, sp_ref, sc_ref, yatt_ref, pw_ref, ps_ref, cw_ref, gp_ref, gc_ref, ga_ref,
                       mixa_ref, mixb_ref, pstate_ref, cstate_ref):
    u = proj_ref[:, 0:POOL_WIDTH]
    hb = proj_ref[:, POOL_WIDTH:POOL_WIDTH + CONV_WIDTH]
    hc = proj_ref[:, POOL_WIDTH + CONV_WIDTH:POOL_WIDTH + 2 * CONV_WIDTH]
    hx = proj_ref[:, POOL_WIDTH + 2 * CONV_WIDTH:POOL_WIDTH + 3 * CONV_WIDTH]

    ys = []
    for g, win in enumerate(POOL_WINDOWS):
        cols = slice(g * POOL_GROUP_DIM, (g + 1) * POOL_GROUP_DIM)
        ug = u[:, cols]
        acc = ug
        for j in range(1, win):
            acc = acc + sp_ref[POOL_STATE - j, :, cols]
        cnt = float(min(POOL_STATE + 1, win))
        pooled = (acc / cnt - ug).astype(BF16)
        ys.append(jnp.dot(pooled, pw_ref[g], preferred_element_type=F32))
    y_pool = jnp.concatenate(ys, axis=1) * ps_ref[...]
    mixa_ref[:, 0:POOL_WIDTH] = _rms(y_pool, gp_ref[...]).astype(BF16)

    z = hc * hx
    conv = cw_ref[0:1, :] * sc_ref[0] + cw_ref[1:2, :] * sc_ref[1] + cw_ref[2:3, :] * z
    mixa_ref[:, POOL_WIDTH:POOL_WIDTH + CONV_WIDTH] = _rms(hb * conv, gc_ref[...]).astype(BF16)
    mixb_ref[...] = _rms(yatt_ref[...], ga_ref[...]).astype(BF16)

    for r in range(POOL_STATE - 1):
        pstate_ref[r] = sp_ref[r + 1]
    pstate_ref[POOL_STATE - 1] = u
    cstate_ref[0] = sc_ref[1]
    cstate_ref[1] = z


def _sample_mixers(proj, sp_t, sc_t, y_att, pool_w_b, pool_scale, conv_w, g_pool, g_conv, g_att):
    db = proj.shape[0]
    return pl.pallas_call(
        _sample_mix_kernel,
        out_shape=(jax.ShapeDtypeStruct((db, POOL_WIDTH + CONV_WIDTH), BF16),
                   jax.ShapeDtypeStruct((db, ATT_WIDTH), BF16),
                   jax.ShapeDtypeStruct((POOL_STATE, db, POOL_WIDTH), F32),
                   jax.ShapeDtypeStruct((CONV_K - 1, db, CONV_WIDTH), F32)),
        compiler_params=pltpu.CompilerParams(vmem_limit_bytes=VMEM_LIMIT),
        name="sample_mixers",
    )(proj, sp_t, sc_t, y_att, pool_w_b, pool_scale, conv_w, g_pool, g_conv, g_att)


def kernel(x_prompt, x_sample, cache_k, cache_v, cache_logf, state_pool, state_conv, state_ffn, page_table,
           norm1_g, w_in, b_f, pool_w, pool_scale, conv_w, q_norm_g, k_norm_g, out_norm_g, w_o, norm2_g,
           w_up, ffn_conv_w, ffn_conv_b, w_down):
    nb, s, _ = x_prompt.shape
    db = x_sample.shape[0]
    depth = w_in.shape[0]
    n_pool = cache_k.shape[1]

    xp = x_prompt.reshape(nb * s, D_MODEL)
    xs = x_sample.reshape(db, D_MODEL)
    logf_flat = cache_logf.reshape(depth, n_pool, PAGE_ROW)
    zero_prev = jnp.zeros((1, 1, D_FF), F32)

    outs = {name: [] for name in ("kp", "vp", "fp", "pp", "cp", "ffp", "ks", "vs", "fs", "ps", "cs", "ffs")}
    for l in range(depth):
        w_in_b = w_in[l].astype(BF16)
        wf_b = jnp.pad(w_in[l, :, D_PROJ:], ((0, 0), (0, LANES - N_HEADS))).astype(BF16)
        bf_pad = jnp.pad(b_f[l], (0, LANES - N_HEADS)).reshape(1, LANES)
        pool_w_b = pool_w[l].astype(BF16)
        w_o_b = w_o[l].astype(BF16)
        w_up_b = w_up[l].astype(BF16)
        w_down_b = w_down[l].astype(BF16)
        g1 = norm1_g[l].reshape(1, D_MODEL)
        g2 = norm2_g[l].reshape(1, D_MODEL)
        qg = q_norm_g[l].reshape(1, HEAD_DIM)
        kg = k_norm_g[l].reshape(1, HEAD_DIM)
        ps_l = pool_scale[l].reshape(1, POOL_WIDTH)
        g_pool = out_norm_g[l, :POOL_WIDTH].reshape(1, POOL_WIDTH)
        g_conv = out_norm_g[l, POOL_WIDTH:POOL_WIDTH + CONV_WIDTH].reshape(1, CONV_WIDTH)
        g_att = out_norm_g[l, POOL_WIDTH + CONV_WIDTH:].reshape(1, ATT_WIDTH)
        cb = ffn_conv_b[l].reshape(1, D_FF)

        proj, qb, k32, kb, v32, vb, logf, logf_pad = _inproj(xp, g1, w_in_b, wf_b, bf_pad, qg, kg, tm=1024)
        ccol, crow = _cumsum(logf_pad, nb, s)
        mix_a, pstate, cstate = _mixers(proj, pool_w_b, ps_l, conv_w[l], g_pool, g_conv, nb, s)
        mix_b = _fox_prompt(qb, kb, vb, ccol, crow, g_att, nb, s)
        x_mid = _outproj(xp, mix_a, mix_b, w_o_b, tm=1024)
        xp, fstate = _ffn(x_mid, g2, w_up_b, ffn_conv_w[l], cb, w_down_b, zero_prev,
                          tm=512, seq_len=s, per_row_state=False)
        outs["kp"].append(k32.reshape(nb, s, N_HEADS, HEAD_DIM))
        outs["vp"].append(v32.reshape(nb, s, N_HEADS, HEAD_DIM))
        outs["fp"].append(logf.reshape(nb, s, N_HEADS))
        outs["pp"].append(pstate)
        outs["cp"].append(cstate)
        outs["ffp"].append(fstate)

        proj_s, qb_s, k32_s, _, v32_s, _, logf_s, _ = _inproj(xs, g1, w_in_b, wf_b, bf_pad, qg, kg, tm=db)
        lfnew_rep = jnp.tile(logf_s, (1, PAGE_SIZE)).reshape(db, 1, PAGE_ROW)
        bias = _sample_bias(page_table, logf_flat[l], lfnew_rep)
        y_att = _paged_attention(l, page_table, qb_s.reshape(db, N_HEADS, HEAD_DIM), bias,
                                 k32_s.reshape(db, N_HEADS, HEAD_DIM), v32_s.reshape(db, N_HEADS, HEAD_DIM),
                                 cache_k, cache_v)
        mix_a_s, mix_b_s, pstate_t, cstate_t = _sample_mixers(
            proj_s, jnp.swapaxes(state_pool[l], 0, 1), jnp.swapaxes(state_conv[l], 0, 1),
            y_att.reshape(db, ATT_WIDTH), pool_w_b, ps_l, conv_w[l], g_pool, g_conv, g_att)
        xs_mid = _outproj(xs, mix_a_s, mix_b_s, w_o_b, tm=db)
        xs, fstate_t = _ffn(xs_mid, g2, w_up_b, ffn_conv_w[l], cb, w_down_b, jnp.swapaxes(state_ffn[l], 0, 1),
                            tm=db, seq_len=1, per_row_state=True)
        outs["ks"].append(k32_s.reshape(db, 1, N_HEADS, HEAD_DIM))
        outs["vs"].append(v32_s.reshape(db, 1, N_HEADS, HEAD_DIM))
        outs["fs"].append(logf_s.reshape(db, 1, N_HEADS))
        outs["ps"].append(jnp.swapaxes(pstate_t, 0, 1))
        outs["cs"].append(jnp.swapaxes(cstate_t, 0, 1))
        outs["ffs"].append(jnp.swapaxes(fstate_t, 0, 1))

    st = lambda name: jnp.stack(outs[name])
    return (xp.reshape(nb, s, D_MODEL), xs.reshape(db, 1, D_MODEL),
            st("kp"), st("vp"), st("fp"), st("pp"), st("cp"), st("ffp"),
            st("ks"), st("vs"), st("fs"), st("ps"), st("cs"), st("ffs"))
```

```python
import functools

import jax
import jax.numpy as jnp
from jax import lax
from jax.experimental import pallas as pl
from jax.experimental.pallas import tpu as pltpu

F32 = jnp.float32
BF16 = jnp.bfloat16

D_MODEL = 2048
PAGE_SIZE = 128
POOL_WIDTH = 512
POOL_WINDOWS = (2, 4, 8, 16)
POOL_GROUP_DIM = 128
POOL_STATE = 15
CONV_WIDTH = 512
CONV_K = 3
ATT_WIDTH = 1024
HEAD_DIM = 128
N_HEADS = 8
D_FF = 5632
FFN_K = 3
EPS = 1e-6
D_PROJ = POOL_WIDTH + 3 * CONV_WIDTH + 3 * ATT_WIDTH
SCALE = HEAD_DIM ** -0.5

LANES = 128
SUBLANES = 8
VMEM_LIMIT = 56 * 1024 * 1024

PROJ_TN = 512
PROMPT_FFN_TM = 1024
PAGES_PER_STEP = 8


def _cparams(sem):
    return pltpu.CompilerParams(dimension_semantics=sem, vmem_limit_bytes=VMEM_LIMIT)


def _rms(x, g):
    ms = jnp.mean(x * x, axis=-1, keepdims=True)
    return x * lax.rsqrt(ms + EPS) * g


def _log_sigmoid(z):
    return jnp.minimum(z, 0.0) - jnp.log1p(jnp.exp(-jnp.abs(z)))


def _split3(x):
    p0 = x.astype(BF16)
    r = x - p0.astype(F32)
    p1 = r.astype(BF16)
    p2 = (r - p1.astype(F32)).astype(BF16)
    return p0, p1, p2


def _dot01(a_pieces, b):
    out = jnp.dot(a_pieces[0], b, preferred_element_type=F32)
    for p in a_pieces[1:]:
        out = out + jnp.dot(p, b, preferred_element_type=F32)
    return out


def _inproj_kernel(x_ref, g1_ref, w_ref, wf_ref, bf_ref, qg_ref, kg_ref,
                   proj_ref, q_ref, k32_ref, kb_ref, v32_ref, vb_ref, logf_ref, logfp_ref, xn_ref, *, tm):
    n = pl.program_id(1)
    rc = min(tm, 256)

    @pl.when(n == 0)
    def _():
        def body(i, c):
            r = pl.ds(pl.multiple_of(i * rc, rc), rc)
            xn_ref[r, :] = _rms(x_ref[r, :], g1_ref[...]).astype(BF16)
            return c
        lax.fori_loop(0, tm // rc, body, 0)
        fl = jnp.dot(xn_ref[...], wf_ref[...], preferred_element_type=F32)
        lf = _log_sigmoid(fl + bf_ref[...])
        lane = lax.broadcasted_iota(jnp.int32, lf.shape, 1)
        lf = jnp.where(lane < N_HEADS, lf, 0.0)
        logfp_ref[...] = lf
        logf_ref[...] = lf[:, :N_HEADS]

    y = jnp.dot(xn_ref[...], w_ref[...], preferred_element_type=F32)

    def head_norm(g):
        parts = []
        for h in range(PROJ_TN // HEAD_DIM):
            parts.append(_rms(y[:, h * HEAD_DIM:(h + 1) * HEAD_DIM], g))
        return jnp.concatenate(parts, axis=1)

    @pl.when(n < 4)
    def _():
        proj_ref[...] = y

    @pl.when((n >= 4) & (n < 6))
    def _():
        q_ref[...] = (head_norm(qg_ref[...]) * SCALE).astype(BF16)

    @pl.when((n >= 6) & (n < 8))
    def _():
        kn = head_norm(kg_ref[...])
        k32_ref[...] = kn
        kb_ref[...] = kn.astype(BF16)

    @pl.when(n >= 8)
    def _():
        v32_ref[...] = y
        vb_ref[...] = y.astype(BF16)


def _inproj(layer, x, g1, w_in_b, bf_pad, qg, kg, tm):
    m = x.shape[0]
    nt = D_PROJ // PROJ_TN
    col = lambda lo: (lambda i, n: (i, jnp.clip(n - lo, 0, 1)))
    out_shape = (
        jax.ShapeDtypeStruct((m, 4 * PROJ_TN), F32),
        jax.ShapeDtypeStruct((m, ATT_WIDTH), BF16),
        jax.ShapeDtypeStruct((m, ATT_WIDTH), F32),
        jax.ShapeDtypeStruct((m, ATT_WIDTH), BF16),
        jax.ShapeDtypeStruct((m, ATT_WIDTH), F32),
        jax.ShapeDtypeStruct((m, ATT_WIDTH), BF16),
        jax.ShapeDtypeStruct((m, N_HEADS), F32),
        jax.ShapeDtypeStruct((m, LANES), F32),
    )
    out_specs = (
        pl.BlockSpec((tm, PROJ_TN), lambda i, n: (i, jnp.minimum(n, 3))),
        pl.BlockSpec((tm, PROJ_TN), col(4)),
        pl.BlockSpec((tm, PROJ_TN), col(6)),
        pl.BlockSpec((tm, PROJ_TN), col(6)),
        pl.BlockSpec((tm, PROJ_TN), col(8)),
        pl.BlockSpec((tm, PROJ_TN), col(8)),
        pl.BlockSpec((tm, N_HEADS), lambda i, n: (i, 0)),
        pl.BlockSpec((tm, LANES), lambda i, n: (i, 0)),
    )
    in_specs = [
        pl.BlockSpec((tm, D_MODEL), lambda i, n: (i, 0)),
        pl.BlockSpec((1, D_MODEL), lambda i, n: (0, 0)),
        pl.BlockSpec((None, D_MODEL, PROJ_TN), lambda i, n: (layer, 0, n)),
        pl.BlockSpec((None, D_MODEL, LANES), lambda i, n: (layer, 0, D_PROJ // LANES)),
        pl.BlockSpec((1, LANES), lambda i, n: (0, 0)),
        pl.BlockSpec((1, HEAD_DIM), lambda i, n: (0, 0)),
        pl.BlockSpec((1, HEAD_DIM), lambda i, n: (0, 0)),
    ]
    return pl.pallas_call(
        functools.partial(_inproj_kernel, tm=tm),
        grid=(m // tm, nt),
        in_specs=in_specs, out_specs=out_specs, out_shape=out_shape,
        scratch_shapes=[pltpu.VMEM((tm, D_MODEL), BF16)],
        compiler_params=_cparams(("arbitrary", "arbitrary")),
        name="inproj",
    )(x, g1, w_in_b, w_in_b, bf_pad, qg, kg)


CUM_BLK = 256


def _cumsum_kernel(lf_ref, ccol_ref, crow_ref, *, s):
    r = lax.broadcasted_iota(jnp.int32, (CUM_BLK, CUM_BLK), 0)
    c = lax.broadcasted_iota(jnp.int32, (CUM_BLK, CUM_BLK), 1)
    tri = (c <= r).astype(BF16)
    carry = jnp.zeros((1, LANES), F32)
    for b in range(s // CUM_BLK):
        rows = slice(b * CUM_BLK, (b + 1) * CUM_BLK)
        lf = lf_ref[0, rows, :]
        pieces = _split3(lf)
        cs = jnp.dot(tri, pieces[0], preferred_element_type=F32)
        cs = cs + jnp.dot(tri, pieces[1], preferred_element_type=F32)
        cs = cs + jnp.dot(tri, pieces[2], preferred_element_type=F32)
        cs = cs + carry
        carry = cs[CUM_BLK - 1:CUM_BLK, :]
        ccol_ref[0, rows, :] = cs[:, :N_HEADS]
        crow_ref[0, :, rows] = cs.T[:N_HEADS, :]


def _cumsum(logf_pad, nseq, s):
    lf3 = logf_pad.reshape(nseq, s, LANES)
    return pl.pallas_call(
        functools.partial(_cumsum_kernel, s=s),
        grid=(nseq,),
        in_specs=[pl.BlockSpec((1, s, LANES), lambda b: (b, 0, 0))],
        out_specs=(pl.BlockSpec((1, s, N_HEADS), lambda b: (b, 0, 0)),
                   pl.BlockSpec((1, N_HEADS, s), lambda b: (b, 0, 0))),
        out_shape=(jax.ShapeDtypeStruct((nseq, s, N_HEADS), F32),
                   jax.ShapeDtypeStruct((nseq, N_HEADS, s), F32)),
        compiler_params=_cparams(("arbitrary",)),
        name="logf_cumsum",
    )(lf3)


POOL_HALO = 16
CONV_HALO = 8


def _mix_kernel(u_ref, hb_ref, hc_ref, hx_ref, pw_ref, ps_ref, cw_ref, gp_ref, gc_ref,
                mix_ref, pstate_ref, cstate_ref, ubuf, zbuf, *, ts):
    si = pl.program_id(1)

    @pl.when(si == 0)
    def _():
        ubuf[0:POOL_HALO, :] = jnp.zeros((POOL_HALO, POOL_WIDTH), F32)
        zbuf[0:CONV_HALO, :] = jnp.zeros((CONV_HALO, CONV_WIDTH), F32)

    @pl.when(si > 0)
    def _():
        ubuf[0:POOL_HALO, :] = ubuf[ts:ts + POOL_HALO, :]
        zbuf[0:CONV_HALO, :] = zbuf[ts:ts + CONV_HALO, :]

    u = u_ref[...]
    ubuf[POOL_HALO:POOL_HALO + ts, :] = u
    z = hc_ref[...] * hx_ref[...]
    zbuf[CONV_HALO:CONV_HALO + ts, :] = z

    pos = si * ts + lax.broadcasted_iota(jnp.int32, (ts, 1), 0)
    ys = []
    for g, win in enumerate(POOL_WINDOWS):
        cols = slice(g * POOL_GROUP_DIM, (g + 1) * POOL_GROUP_DIM)
        ug = u[:, cols]
        acc = ug
        for j in range(1, win):
            acc = acc + ubuf[POOL_HALO - j:POOL_HALO - j + ts, cols]
        cnt = jnp.minimum(pos + 1, win).astype(F32)
        pooled = (acc / cnt - ug).astype(BF16)
        ys.append(jnp.dot(pooled, pw_ref[g], preferred_element_type=F32))
    y_pool = jnp.concatenate(ys, axis=1) * ps_ref[...]
    mix_ref[:, 0:POOL_WIDTH] = _rms(y_pool, gp_ref[...]).astype(BF16)

    conv = (cw_ref[0:1, :] * zbuf[CONV_HALO - 2:CONV_HALO - 2 + ts, :]
            + cw_ref[1:2, :] * zbuf[CONV_HALO - 1:CONV_HALO - 1 + ts, :]
            + cw_ref[2:3, :] * z)
    y_conv = hb_ref[...] * conv
    mix_ref[:, POOL_WIDTH:POOL_WIDTH + CONV_WIDTH] = _rms(y_conv, gc_ref[...]).astype(BF16)

    pstate_ref[0] = ubuf[POOL_HALO + ts - POOL_STATE:POOL_HALO + ts, :]
    cstate_ref[0] = zbuf[CONV_HALO + ts - (CONV_K - 1):CONV_HALO + ts, :]


def _mixers(proj, pool_w_b, pool_scale, conv_w, g_pool, g_conv, nseq, s, ts=512):
    m = proj.shape[0]
    nst = s // ts
    pcol = lambda j: pl.BlockSpec((ts, PROJ_TN), lambda b, i: (b * nst + i, j))
    const = lambda shape: pl.BlockSpec(shape, lambda b, i: (0,) * len(shape))
    return pl.pallas_call(
        functools.partial(_mix_kernel, ts=ts),
        grid=(nseq, nst),
        in_specs=[pcol(0), pcol(1), pcol(2), pcol(3),
                  const((len(POOL_WINDOWS), POOL_GROUP_DIM, POOL_GROUP_DIM)),
                  const((1, POOL_WIDTH)), const((CONV_K, CONV_WIDTH)),
                  const((1, POOL_WIDTH)), const((1, CONV_WIDTH))],
        out_specs=(pl.BlockSpec((ts, POOL_WIDTH + CONV_WIDTH), lambda b, i: (b * nst + i, 0)),
                   pl.BlockSpec((1, POOL_STATE, POOL_WIDTH), lambda b, i: (b, 0, 0)),
                   pl.BlockSpec((1, CONV_K - 1, CONV_WIDTH), lambda b, i: (b, 0, 0))),
        out_shape=(jax.ShapeDtypeStruct((m, POOL_WIDTH + CONV_WIDTH), BF16),
                   jax.ShapeDtypeStruct((nseq, POOL_STATE, POOL_WIDTH), F32),
                   jax.ShapeDtypeStruct((nseq, CONV_K - 1, CONV_WIDTH), F32)),
        scratch_shapes=[pltpu.VMEM((POOL_HALO + ts, POOL_WIDTH), F32),
                        pltpu.VMEM((CONV_HALO + ts, CONV_WIDTH), F32)],
        compiler_params=_cparams(("arbitrary", "arbitrary")),
        name="prompt_mixers",
    )(proj, proj, proj, proj, pool_w_b, pool_scale, conv_w, g_pool, g_conv)


def _fox_kernel(q_ref, k_ref, v_ref, cq_ref, ck_ref, g_ref, out_ref, m_scr, l_scr, cq_scr, acc_scr, *, tq, tk):
    i = pl.program_id(1)
    nrep = tk // LANES
    rel = (lax.broadcasted_iota(jnp.int32, (tq, tk), 1) - lax.broadcasted_iota(jnp.int32, (tq, tk), 0))
    ones = jnp.ones((tk, HEAD_DIM), BF16)
    lanes = lambda x: jnp.concatenate([x] * nrep, axis=1)

    for h in range(N_HEADS):
        m_scr[h] = jnp.full((tq, LANES), -jnp.inf, F32)
        l_scr[h] = jnp.zeros((tq, LANES), F32)
        cq_scr[h] = jnp.broadcast_to(cq_ref[0, :, h:h + 1], (tq, LANES))
    acc_scr[...] = jnp.zeros(acc_scr.shape, F32)

    def step(j, masked):
        ks = pl.ds(pl.multiple_of(j * tk, tk), tk)
        for h in range(N_HEADS):
            cs = slice(h * HEAD_DIM, (h + 1) * HEAD_DIM)
            s = lax.dot_general(q_ref[:, cs], k_ref[ks, cs], (((1,), (1,)), ((), ())),
                                preferred_element_type=F32)
            s = s + lanes(cq_scr[h]) - ck_ref[0, h:h + 1, ks]
            if masked:
                s = jnp.where(rel <= i * tq - j * tk, s, -jnp.inf)
            m_prev = m_scr[h]
            m_new = jnp.maximum(m_prev, jnp.broadcast_to(jnp.max(s, axis=1, keepdims=True), (tq, LANES)))
            alpha = jnp.exp(m_prev - m_new)
            p = jnp.exp(s - lanes(m_new)).astype(BF16)
            pv = jnp.dot(p, jnp.concatenate([v_ref[ks, cs], ones], axis=1), preferred_element_type=F32)
            l_scr[h] = alpha * l_scr[h] + pv[:, HEAD_DIM:]
            acc_scr[:, cs] = alpha * acc_scr[:, cs] + pv[:, :HEAD_DIM]
            m_scr[h] = m_new

    def body(j, c):
        step(j, False)
        return c

    n_full = (i * tq) // tk
    lax.fori_loop(0, n_full, body, 0)
    step(n_full, True)

    for h in range(N_HEADS):
        cs = slice(h * HEAD_DIM, (h + 1) * HEAD_DIM)
        acc_scr[:, cs] = acc_scr[:, cs] / l_scr[h]
    out_ref[...] = _rms(acc_scr[...], g_ref[...]).astype(BF16)


def _fox_prompt(qb, kb, vb, ccol, crow, g_att, nseq, s, tq=256, tk=512):
    assert tk % tq == 0 and s % tk == 0
    m = qb.shape[0]
    nq = s // tq
    stat = pltpu.VMEM((N_HEADS, tq, LANES), F32)
    return pl.pallas_call(
        functools.partial(_fox_kernel, tq=tq, tk=tk),
        grid=(nseq, nq),
        in_specs=[pl.BlockSpec((tq, ATT_WIDTH), lambda b, i: (b * nq + i, 0)),
                  pl.BlockSpec((s, ATT_WIDTH), lambda b, i: (b, 0)),
                  pl.BlockSpec((s, ATT_WIDTH), lambda b, i: (b, 0)),
                  pl.BlockSpec((1, tq, N_HEADS), lambda b, i: (b, i, 0)),
                  pl.BlockSpec((1, N_HEADS, s), lambda b, i: (b, 0, 0)),
                  pl.BlockSpec((1, ATT_WIDTH), lambda b, i: (0, 0))],
        out_specs=pl.BlockSpec((tq, ATT_WIDTH), lambda b, i: (b * nq + i, 0)),
        out_shape=jax.ShapeDtypeStruct((m, ATT_WIDTH), BF16),
        scratch_shapes=[stat, stat, stat, pltpu.VMEM((tq, ATT_WIDTH), F32)],
        compiler_params=_cparams(("arbitrary", "arbitrary")),
        name="fox_prompt",
    )(qb, kb, vb, ccol, crow, g_att)


def _outproj_kernel(x_ref, ma_ref, mb_ref, wa_ref, wb_ref, o_ref):
    acc = jnp.dot(ma_ref[...], wa_ref[...], preferred_element_type=F32)
    acc = acc + jnp.dot(mb_ref[...], wb_ref[...], preferred_element_type=F32)
    o_ref[...] = x_ref[...] + acc


def _outproj(layer, x, mix_a, mix_b, w_o_b, tm, tn=512):
    m = x.shape[0]
    half = D_MODEL // 2
    return pl.pallas_call(
        _outproj_kernel,
        grid=(m // tm, D_MODEL // tn),
        in_specs=[pl.BlockSpec((tm, tn), lambda i, n: (i, n)),
                  pl.BlockSpec((tm, half), lambda i, n: (i, 0)),
                  pl.BlockSpec((tm, half), lambda i, n: (i, 0)),
                  pl.BlockSpec((None, half, tn), lambda i, n: (layer, 0, n)),
                  pl.BlockSpec((None, half, tn), lambda i, n: (layer, 1, n))],
        out_specs=pl.BlockSpec((tm, tn), lambda i, n: (i, n)),
        out_shape=jax.ShapeDtypeStruct((m, D_MODEL), F32),
        compiler_params=_cparams(("arbitrary", "arbitrary")),
        name="outproj",
    )(x, mix_a, mix_b, w_o_b, w_o_b)


FFN_HALO = 8


def _ffn_kernel(*refs, tm, nf, tiles_per_seq, per_row_state):
    if per_row_state:
        (x_ref, g2_ref, wa_ref, wg_ref, cw_ref, cb_ref, wd_ref, prev_ref,
         o_ref, state_ref, xn_ref, act_scr, abuf, carry_ref) = refs
    else:
        (x_ref, g2_ref, wa_ref, wg_ref, cw_ref, cb_ref, wd_ref,
         o_ref, state_ref, xn_ref, act_scr, abuf, carry_ref) = refs
    mi = pl.program_id(0)
    f = pl.program_id(1)
    rc = min(tm, 256)

    @pl.when(f == 0)
    def _():
        def body(i, c):
            r = pl.ds(pl.multiple_of(i * rc, rc), rc)
            x = x_ref[r, :]
            xn_ref[r, :] = _rms(x, g2_ref[...]).astype(BF16)
            o_ref[r, :] = x
            return c
        lax.fori_loop(0, tm // rc, body, 0)

        @pl.when(mi == 0)
        def _():
            carry_ref[...] = jnp.zeros(carry_ref.shape, F32)

    def up():
        xn = xn_ref[...]
        a = jnp.dot(xn, wa_ref[...], preferred_element_type=F32)
        gate = jnp.dot(xn, wg_ref[...], preferred_element_type=F32)
        if per_row_state:
            p1 = prev_ref[1]
            ac = cw_ref[0:1, :] * prev_ref[0] + cw_ref[1:2, :] * p1 + cw_ref[2:3, :] * a
            state_ref[0] = p1
            state_ref[1] = a
        else:
            first = (mi % tiles_per_seq) == 0
            abuf[0:FFN_HALO, :] = jnp.where(first, 0.0, carry_ref[f])
            abuf[FFN_HALO:FFN_HALO + tm, :] = a
            carry_ref[f] = a[tm - FFN_HALO:tm, :]
            ac = (cw_ref[0:1, :] * abuf[FFN_HALO - 2:FFN_HALO - 2 + tm, :]
                  + cw_ref[1:2, :] * abuf[FFN_HALO - 1:FFN_HALO - 1 + tm, :]
                  + cw_ref[2:3, :] * a)
            state_ref[0] = a[tm - (FFN_K - 1):tm, :]
        ac = ac + cb_ref[...]
        act_scr[f % 2] = (ac * jax.nn.sigmoid(ac) * gate).astype(BF16)

    def down():
        o_ref[...] += jnp.dot(act_scr[(f + 1) % 2], wd_ref[...], preferred_element_type=F32)

    @pl.when(f == 0)
    def _():
        up()

    @pl.when((f > 0) & (f < nf))
    def _():
        down()
        up()

    @pl.when(f == nf)
    def _():
        down()


def _ffn(layer, x, g2, w_up_b, cw, cb, w_down_b, prev, tm, seq_len, tf=512):
    m = x.shape[0]
    nf = D_FF // tf
    per_row_state = prev is not None
    cur = lambda f: jnp.minimum(f, nf - 1)
    old = lambda f: jnp.maximum(f - 1, 0)
    in_specs = [pl.BlockSpec((tm, D_MODEL), lambda i, f: (i, 0), pipeline_mode=pl.Buffered(1)),
                pl.BlockSpec((1, D_MODEL), lambda i, f: (0, 0)),
                pl.BlockSpec((None, D_MODEL, tf), lambda i, f: (layer, 0, cur(f))),
                pl.BlockSpec((None, D_MODEL, tf), lambda i, f: (layer, 0, nf + cur(f))),
                pl.BlockSpec((FFN_K, tf), lambda i, f: (0, cur(f))),
                pl.BlockSpec((1, tf), lambda i, f: (0, cur(f))),
                pl.BlockSpec((None, tf, D_MODEL), lambda i, f: (layer, old(f), 0))]
    args = [x, g2, w_up_b, w_up_b, cw, cb, w_down_b]
    if per_row_state:
        tiles_per_seq = 1
        in_specs.append(pl.BlockSpec((FFN_K - 1, tm, tf), lambda i, f: (0, i, cur(f))))
        args.append(prev)
        state_spec = pl.BlockSpec((FFN_K - 1, tm, tf), lambda i, f: (0, i, cur(f)))
        state_shape = jax.ShapeDtypeStruct((FFN_K - 1, m, D_FF), F32)
    else:
        tiles_per_seq = seq_len // tm
        state_spec = pl.BlockSpec((1, FFN_K - 1, tf), lambda i, f: (i, 0, cur(f)))
        state_shape = jax.ShapeDtypeStruct((m // tm, FFN_K - 1, D_FF), F32)
    x_out, state = pl.pallas_call(
        functools.partial(_ffn_kernel, tm=tm, nf=nf, tiles_per_seq=tiles_per_seq, per_row_state=per_row_state),
        grid=(m // tm, nf + 1),
        in_specs=in_specs,
        out_specs=(pl.BlockSpec((tm, D_MODEL), lambda i, f: (i, 0)), state_spec),
        out_shape=(jax.ShapeDtypeStruct((m, D_MODEL), F32), state_shape),
        scratch_shapes=[pltpu.VMEM((tm, D_MODEL), BF16),
                        pltpu.VMEM((2, tm, tf), BF16),
                        pltpu.VMEM((FFN_HALO + tm, tf), F32),
                        pltpu.VMEM((nf, FFN_HALO, tf), F32)],
        compiler_params=_cparams(("arbitrary", "arbitrary")),
        name="ffn",
    )(*args)
    if not per_row_state:
        state = state[tiles_per_seq - 1::tiles_per_seq]
    return x_out, state


PAGE_ROW = PAGE_SIZE * N_HEADS


def _bias_kernel(pt_ref, lf_hbm, lfnew_ref, bias_ref, lfbuf, msuf_scr, mtot_scr, sem, *, n_pages):
    b = pl.program_id(0)

    @pl.when(b == 0)
    def _():
        src = lax.broadcasted_iota(jnp.int32, (PAGE_ROW, PAGE_ROW), 0)
        dst = lax.broadcasted_iota(jnp.int32, (PAGE_ROW, PAGE_ROW), 1)
        same_head = (src & (N_HEADS - 1)) == (dst & (N_HEADS - 1))
        later = (src >> 3) > (dst >> 3)
        msuf_scr[...] = (same_head & later).astype(BF16)
        mtot_scr[...] = same_head.astype(BF16)

    def copy(p):
        page = pt_ref[b, p]
        return pltpu.make_async_copy(lf_hbm.at[pl.ds(page, 1)], lfbuf.at[pl.ds(p, 1)], sem)

    def start(p, c):
        copy(p).start()
        return c

    def wait(p, c):
        copy(p).wait()
        return c

    lax.fori_loop(0, n_pages, start, 0)
    lax.fori_loop(0, n_pages, wait, 0)

    m_suffix = msuf_scr[...]
    m_total = mtot_scr[...]
    pr = lax.broadcasted_iota(jnp.int32, (n_pages, n_pages), 0)
    pc = lax.broadcasted_iota(jnp.int32, (n_pages, n_pages), 1)
    later_page = (pc > pr).astype(BF16)

    pieces = _split3(lfbuf[...])
    within = _dot01(pieces, m_suffix)
    total = _dot01(pieces, m_total)
    t0, t1, t2 = _split3(total)
    after = jnp.dot(later_page, t0, preferred_element_type=F32)
    after = after + jnp.dot(later_page, t1, preferred_element_type=F32)
    after = after + jnp.dot(later_page, t2, preferred_element_type=F32)
    bias_ref[0] = within + after + lfnew_ref[0]


def _sample_bias(page_table, logf_flat, lfnew_rep):
    db, n_pages = page_table.shape
    grid_spec = pltpu.PrefetchScalarGridSpec(
        num_scalar_prefetch=1,
        grid=(db,),
        in_specs=[pl.BlockSpec(memory_space=pl.ANY),
                  pl.BlockSpec((1, 1, PAGE_ROW), lambda b, pt: (b, 0, 0))],
        out_specs=pl.BlockSpec((1, n_pages, PAGE_ROW), lambda b, pt: (b, 0, 0)),
        scratch_shapes=[pltpu.VMEM((n_pages, PAGE_ROW), F32),
                        pltpu.VMEM((PAGE_ROW, PAGE_ROW), BF16), pltpu.VMEM((PAGE_ROW, PAGE_ROW), BF16),
                        pltpu.SemaphoreType.DMA(())],
    )
    return pl.pallas_call(
        functools.partial(_bias_kernel, n_pages=n_pages),
        grid_spec=grid_spec,
        out_shape=jax.ShapeDtypeStruct((db, n_pages, PAGE_ROW), F32),
        compiler_params=_cparams(("arbitrary",)),
        name="sample_bias",
    )(page_table, logf_flat, lfnew_rep)


def _head_allreduce(x, op, reduce_op):
    x = jnp.broadcast_to(reduce_op(x, axis=0, keepdims=True), (SUBLANES, LANES))
    for sh in (8, 16, 32, 64):
        x = op(x, pltpu.roll(x, sh, axis=1))
    return x


def _paged_kernel(pt_ref, q_ref, bias_ref, knew_ref, vnew_ref, *refs, n_steps):
    k_refs = refs[:PAGES_PER_STEP]
    v_refs = refs[PAGES_PER_STEP:2 * PAGES_PER_STEP]
    o_ref = refs[2 * PAGES_PER_STEP]
    m_scr, l_scr, acc_scr, s_scr = refs[2 * PAGES_PER_STEP + 1:]
    i = pl.program_id(1)

    @pl.when(i == 0)
    def _():
        m_scr[...] = jnp.full((SUBLANES, LANES), -jnp.inf, F32)
        l_scr[...] = jnp.zeros((SUBLANES, LANES), F32)
        acc_scr[...] = jnp.zeros((N_HEADS, HEAD_DIM), F32)

    q = q_ref[0]
    sub = lax.broadcasted_iota(jnp.int32, (N_HEADS, PAGE_ROW), 0)
    lane = lax.broadcasted_iota(jnp.int32, (N_HEADS, PAGE_ROW), 1)
    own_head = sub == (lane & (N_HEADS - 1))

    for j in range(PAGES_PER_STEP):
        k2 = k_refs[j][...].reshape(PAGE_ROW, HEAD_DIM).astype(BF16)
        st = lax.dot_general(q, k2, (((1,), (1,)), ((), ())), preferred_element_type=F32)
        s_scr[j:j + 1, :] = jnp.sum(jnp.where(own_head, st, 0.0), axis=0, keepdims=True)

    s = s_scr[...] + bias_ref[0]
    chunks = [s[:, c * LANES:(c + 1) * LANES] for c in range(PAGE_ROW // LANES)]
    mx = chunks[0]
    for c in chunks[1:]:
        mx = jnp.maximum(mx, c)
    m_prev = m_scr[...]
    m_new = jnp.maximum(m_prev, _head_allreduce(mx, jnp.maximum, jnp.max))
    alpha = jnp.exp(m_prev - m_new)
    p = jnp.exp(s - jnp.concatenate([m_new] * (PAGE_ROW // LANES), axis=1))
    ps = p[:, 0:LANES]
    for c in range(1, PAGE_ROW // LANES):
        ps = ps + p[:, c * LANES:(c + 1) * LANES]
    l_scr[...] = alpha * l_scr[...] + _head_allreduce(ps, jnp.add, jnp.sum)
    m_scr[...] = m_new

    o = jnp.zeros((N_HEADS, HEAD_DIM), F32)
    for j in range(PAGES_PER_STEP):
        pj = jnp.where(own_head, jnp.broadcast_to(p[j:j + 1, :], (N_HEADS, PAGE_ROW)), 0.0).astype(BF16)
        v2 = v_refs[j][...].reshape(PAGE_ROW, HEAD_DIM).astype(BF16)
        o = o + jnp.dot(pj, v2, preferred_element_type=F32)

    sub8 = lax.broadcasted_iota(jnp.int32, (SUBLANES, LANES), 0)
    lane8 = lax.broadcasted_iota(jnp.int32, (SUBLANES, LANES), 1)
    diag = sub8 == lane8

    def to_col(x):
        return jnp.sum(jnp.where(diag, x, 0.0), axis=1, keepdims=True)

    acc = to_col(alpha) * acc_scr[...] + o
    acc_scr[...] = acc

    @pl.when(i == n_steps - 1)
    def _():
        m_col = to_col(m_new)
        l_col = to_col(l_scr[...])
        qf = q.astype(F32)
        kn = knew_ref[0].astype(BF16).astype(F32)
        s_new = jnp.sum(qf * kn, axis=1, keepdims=True)
        m_f = jnp.maximum(m_col, s_new)
        a_old = jnp.exp(m_col - m_f)
        p_new = jnp.exp(s_new - m_f)
        o_ref[0] = (acc * a_old + p_new * vnew_ref[0]) / (l_col * a_old + p_new)


def _paged_attention(layer, page_table, q3, bias, k_new3, v_new3, cache_k, cache_v):
    db, n_pages = page_table.shape
    n_steps = n_pages // PAGES_PER_STEP

    def page_spec(j):
        return pl.BlockSpec((None, None, PAGE_SIZE, N_HEADS, HEAD_DIM),
                            lambda b, i, pt: (layer, pt[b, i * PAGES_PER_STEP + j], 0, 0, 0))

    row = pl.BlockSpec((1, N_HEADS, HEAD_DIM), lambda b, i, pt: (b, 0, 0))
    grid_spec = pltpu.PrefetchScalarGridSpec(
        num_scalar_prefetch=1,
        grid=(db, n_steps),
        in_specs=[row,
                  pl.BlockSpec((1, PAGES_PER_STEP, PAGE_ROW), lambda b, i, pt: (b, i, 0)),
                  row, row]
                 + [page_spec(j) for j in range(PAGES_PER_STEP)]
                 + [page_spec(j) for j in range(PAGES_PER_STEP)],
        out_specs=row,
        scratch_shapes=[pltpu.VMEM((SUBLANES, LANES), F32), pltpu.VMEM((SUBLANES, LANES), F32),
                        pltpu.VMEM((N_HEADS, HEAD_DIM), F32), pltpu.VMEM((PAGES_PER_STEP, PAGE_ROW), F32)],
    )
    return pl.pallas_call(
        functools.partial(_paged_kernel, n_steps=n_steps),
        grid_spec=grid_spec,
        out_shape=jax.ShapeDtypeStruct((db, N_HEADS, HEAD_DIM), F32),
        compiler_params=_cparams(("arbitrary", "arbitrary")),
        name="paged_attention",
    )(page_table, q3, bias, k_new3, v_new3, *([cache_k] * PAGES_PER_STEP), *([cache_v] * PAGES_PER_STEP))


def _sample_mix_kernel(proj_ref, sp_ref, sc_ref, yatt_ref, pw_ref, ps_ref, cw_ref, gp_ref, gc_ref, ga_ref,
                       mixa_ref, mixb_ref, pstate_ref, cstate_ref):
    u = proj_ref[:, 0:POOL_WIDTH]
    hb = proj_ref[:, POOL_WIDTH:POOL_WIDTH + CONV_WIDTH]
    hc = proj_ref[:, POOL_WIDTH + CONV_WIDTH:POOL_WIDTH + 2 * CONV_WIDTH]
    hx = proj_ref[:, POOL_WIDTH + 2 * CONV_WIDTH:POOL_WIDTH + 3 * CONV_WIDTH]

    ys = []
    for g, win in enumerate(POOL_WINDOWS):
        cols = slice(g * POOL_GROUP_DIM, (g + 1) * POOL_GROUP_DIM)
        ug = u[:, cols]
        acc = ug
        for j in range(1, win):
            acc = acc + sp_ref[POOL_STATE - j, :, cols]
        cnt = float(min(POOL_STATE + 1, win))
        pooled = (acc / cnt - ug).astype(BF16)
        ys.append(jnp.dot(pooled, pw_ref[g], preferred_element_type=F32))
    y_pool = jnp.concatenate(ys, axis=1) * ps_ref[...]
    mixa_ref[:, 0:POOL_WIDTH] = _rms(y_pool, gp_ref[...]).astype(BF16)

    z = hc * hx
    conv = cw_ref[0:1, :] * sc_ref[0] + cw_ref[1:2, :] * sc_ref[1] + cw_ref[2:3, :] * z
    mixa_ref[:, POOL_WIDTH:POOL_WIDTH + CONV_WIDTH] = _rms(hb * conv, gc_ref[...]).astype(BF16)
    mixb_ref[...] = _rms(yatt_ref[...], ga_ref[...]).astype(BF16)

    for r in range(POOL_STATE - 1):
        pstate_ref[r] = sp_ref[r + 1]
    pstate_ref[POOL_STATE - 1] = u
    cstate_ref[0] = sc_ref[1]
    cstate_ref[1] = z


def _sample_mixers(proj, sp_t, sc_t, y_att, pool_w_b, pool_scale, conv_w, g_pool, g_conv, g_att):
    db = proj.shape[0]
    return pl.pallas_call(
        _sample_mix_kernel,
        out_shape=(jax.ShapeDtypeStruct((db, POOL_WIDTH + CONV_WIDTH), BF16),
                   jax.ShapeDtypeStruct((db, ATT_WIDTH), BF16),
                   jax.ShapeDtypeStruct((POOL_STATE, db, POOL_WIDTH), F32),
                   jax.ShapeDtypeStruct((CONV_K - 1, db, CONV_WIDTH), F32)),
        compiler_params=pltpu.CompilerParams(vmem_limit_bytes=VMEM_LIMIT),
        name="sample_mixers",
    )(proj, sp_t, sc_t, y_att, pool_w_b, pool_scale, conv_w, g_pool, g_conv, g_att)


def kernel(x_prompt, x_sample, cache_k, cache_v, cache_logf, state_pool, state_conv, state_ffn, page_table,
           norm1_g, w_in, b_f, pool_w, pool_scale, conv_w, q_norm_g, k_norm_g, out_norm_g, w_o, norm2_g,
           w_up, ffn_conv_w, ffn_conv_b, w_down):
    nb, s, _ = x_prompt.shape
    db = x_sample.shape[0]
    depth = w_in.shape[0]
    n_pool = cache_k.shape[1]

    xp = x_prompt.reshape(nb * s, D_MODEL)
    xs = x_sample.reshape(db, D_MODEL)
    logf_flat = cache_logf.reshape(depth, n_pool, PAGE_ROW)

    w_in_b = jnp.pad(w_in, ((0, 0), (0, 0), (0, LANES - N_HEADS))).astype(BF16)
    w_o_b = w_o.astype(BF16)
    w_up_b = w_up.astype(BF16)
    w_down_b = w_down.astype(BF16)

    outs = {name: [] for name in ("kp", "vp", "fp", "pp", "cp", "ffp", "ks", "vs", "fs", "ps", "cs", "ffs")}
    for l in range(depth):
        bf_pad = jnp.pad(b_f[l], (0, LANES - N_HEADS)).reshape(1, LANES)
        pool_w_b = pool_w[l].astype(BF16)
        g1 = norm1_g[l].reshape(1, D_MODEL)
        g2 = norm2_g[l].reshape(1, D_MODEL)
        qg = q_norm_g[l].reshape(1, HEAD_DIM)
        kg = k_norm_g[l].reshape(1, HEAD_DIM)
        ps_l = pool_scale[l].reshape(1, POOL_WIDTH)
        g_pool = out_norm_g[l, :POOL_WIDTH].reshape(1, POOL_WIDTH)
        g_conv = out_norm_g[l, POOL_WIDTH:POOL_WIDTH + CONV_WIDTH].reshape(1, CONV_WIDTH)
        g_att = out_norm_g[l, POOL_WIDTH + CONV_WIDTH:].reshape(1, ATT_WIDTH)
        cb = ffn_conv_b[l].reshape(1, D_FF)

        proj, qb, k32, kb, v32, vb, logf, logf_pad = _inproj(l, xp, g1, w_in_b, bf_pad, qg, kg, tm=1024)
        ccol, crow = _cumsum(logf_pad, nb, s)
        mix_a, pstate, cstate = _mixers(proj, pool_w_b, ps_l, conv_w[l], g_pool, g_conv, nb, s)
        mix_b = _fox_prompt(qb, kb, vb, ccol, crow, g_att, nb, s)
        x_mid = _outproj(l, xp, mix_a, mix_b, w_o_b, tm=1024)
        xp, fstate = _ffn(l, x_mid, g2, w_up_b, ffn_conv_w[l], cb, w_down_b, None, tm=PROMPT_FFN_TM, seq_len=s)
        outs["kp"].append(k32.reshape(nb, s, N_HEADS, HEAD_DIM))
        outs["vp"].append(v32.reshape(nb, s, N_HEADS, HEAD_DIM))
        outs["fp"].append(logf.reshape(nb, s, N_HEADS))
        outs["pp"].append(pstate)
        outs["cp"].append(cstate)
        outs["ffp"].append(fstate)

        proj_s, qb_s, k32_s, _, v32_s, _, logf_s, _ = _inproj(l, xs, g1, w_in_b, bf_pad, qg, kg, tm=db)
        lfnew_rep = jnp.tile(logf_s, (1, PAGE_SIZE)).reshape(db, 1, PAGE_ROW)
        bias = _sample_bias(page_table, logf_flat[l], lfnew_rep)
        y_att = _paged_attention(l, page_table, qb_s.reshape(db, N_HEADS, HEAD_DIM), bias,
                                 k32_s.reshape(db, N_HEADS, HEAD_DIM), v32_s.reshape(db, N_HEADS, HEAD_DIM),
                                 cache_k, cache_v)
        mix_a_s, mix_b_s, pstate_t, cstate_t = _sample_mixers(
            proj_s, jnp.swapaxes(state_pool[l], 0, 1), jnp.swapaxes(state_conv[l], 0, 1),
            y_att.reshape(db, ATT_WIDTH), pool_w_b, ps_l, conv_w[l], g_pool, g_conv, g_att)
        xs_mid = _outproj(l, xs, mix_a_s, mix_b_s, w_o_b, tm=db)
        xs, fstate_t = _ffn(l, xs_mid, g2, w_up_b, ffn_conv_w[l], cb, w_down_b,
                            jnp.swapaxes(state_ffn[l], 0, 1), tm=db, seq_len=1)
        outs["ks"].append(k32_s.reshape(db, 1, N_HEADS, HEAD_DIM))
        outs["vs"].append(v32_s.reshape(db, 1, N_HEADS, HEAD_DIM))
        outs["fs"].append(logf_s.reshape(db, 1, N_HEADS))
        outs["ps"].append(jnp.swapaxes(pstate_t, 0, 1))
        outs["cs"].append(jnp.swapaxes(cstate_t, 0, 1))
        outs["ffs"].append(jnp.swapaxes(fstate_t, 0, 1))

    st = lambda name: jnp.stack(outs[name])
    return (xp.reshape(nb, s, D_MODEL), xs.reshape(db, 1, D_MODEL),
            st("kp"), st("vp"), st("fp"), st("pp"), st("cp"), st("ffp"),
            st("ks"), st("vs"), st("fs"), st("ps"), st("cs"), st("ffs"))
```

```python
import functools

import jax
import jax.numpy as jnp
from jax import lax
from jax.experimental import pallas as pl
from jax.experimental.pallas import tpu as pltpu

F32 = jnp.float32
BF16 = jnp.bfloat16

D_MODEL = 2048
PAGE_SIZE = 128
POOL_WIDTH = 512
POOL_WINDOWS = (2, 4, 8, 16)
POOL_GROUP_DIM = 128
POOL_STATE = 15
CONV_WIDTH = 512
CONV_K = 3
ATT_WIDTH = 1024
HEAD_DIM = 128
N_HEADS = 8
D_FF = 5632
FFN_K = 3
EPS = 1e-6
D_PROJ = POOL_WIDTH + 3 * CONV_WIDTH + 3 * ATT_WIDTH
SCALE = HEAD_DIM ** -0.5

LANES = 128
SUBLANES = 8
VMEM_LIMIT = 56 * 1024 * 1024

PROJ_TN = 512
PROMPT_FFN_TM = 1024
PAGES_PER_STEP = 16


def _cparams(sem):
    return pltpu.CompilerParams(dimension_semantics=sem, vmem_limit_bytes=VMEM_LIMIT)


def _rms(x, g):
    ms = jnp.mean(x * x, axis=-1, keepdims=True)
    return x * lax.rsqrt(ms + EPS) * g


def _log_sigmoid(z):
    return jnp.minimum(z, 0.0) - jnp.log1p(jnp.exp(-jnp.abs(z)))


def _split3(x):
    p0 = x.astype(BF16)
    r = x - p0.astype(F32)
    p1 = r.astype(BF16)
    p2 = (r - p1.astype(F32)).astype(BF16)
    return p0, p1, p2


def _dot01(a_pieces, b):
    out = jnp.dot(a_pieces[0], b, preferred_element_type=F32)
    for p in a_pieces[1:]:
        out = out + jnp.dot(p, b, preferred_element_type=F32)
    return out


def _inproj_kernel(x_ref, g1_ref, w_ref, wf_ref, bf_ref, qg_ref, kg_ref, *rest, tm, n_aliased):
    proj_ref, q_ref, k32_ref, kb_ref, v32_ref, vb_ref, logf_ref, logfp_ref, xn_ref = rest[n_aliased:]
    n = pl.program_id(1)
    rc = min(tm, 256)

    @pl.when(n == 0)
    def _():
        def body(i, c):
            r = pl.ds(pl.multiple_of(i * rc, rc), rc)
            xn_ref[r, :] = _rms(x_ref[r, :], g1_ref[...]).astype(BF16)
            return c
        lax.fori_loop(0, tm // rc, body, 0)
        fl = jnp.dot(xn_ref[...], wf_ref[...], preferred_element_type=F32)
        lf = _log_sigmoid(fl + bf_ref[...])
        lane = lax.broadcasted_iota(jnp.int32, lf.shape, 1)
        lf = jnp.where(lane < N_HEADS, lf, 0.0)
        logfp_ref[...] = lf
        logf_ref[...] = lf[:, :N_HEADS]

    y = jnp.dot(xn_ref[...], w_ref[...], preferred_element_type=F32)

    def head_norm(g):
        parts = []
        for h in range(PROJ_TN // HEAD_DIM):
            parts.append(_rms(y[:, h * HEAD_DIM:(h + 1) * HEAD_DIM], g))
        return jnp.concatenate(parts, axis=1)

    @pl.when(n < 4)
    def _():
        proj_ref[...] = y

    @pl.when((n >= 4) & (n < 6))
    def _():
        q_ref[...] = (head_norm(qg_ref[...]) * SCALE).astype(BF16)

    @pl.when((n >= 6) & (n < 8))
    def _():
        kn = head_norm(kg_ref[...])
        k32_ref[...] = kn
        kb_ref[...] = kn.astype(BF16)

    @pl.when(n >= 8)
    def _():
        v32_ref[...] = y
        vb_ref[...] = y.astype(BF16)


def _inproj(layer, x, g1, w_in_b, bf_pad, qg, kg, tm, kv_stack=None):
    m = x.shape[0]
    depth = w_in_b.shape[0]
    nt = D_PROJ // PROJ_TN
    col = lambda lo: (lambda i, n: (i, jnp.clip(n - lo, 0, 1)))
    slab = lambda lo: (lambda i, n: (layer, i, jnp.clip(n - lo, 0, 1)))
    out_shape = (
        jax.ShapeDtypeStruct((m, 4 * PROJ_TN), F32),
        jax.ShapeDtypeStruct((m, ATT_WIDTH), BF16),
        jax.ShapeDtypeStruct((depth, m, ATT_WIDTH), F32),
        jax.ShapeDtypeStruct((m, ATT_WIDTH), BF16),
        jax.ShapeDtypeStruct((depth, m, ATT_WIDTH), F32),
        jax.ShapeDtypeStruct((m, ATT_WIDTH), BF16),
        jax.ShapeDtypeStruct((m, N_HEADS), F32),
        jax.ShapeDtypeStruct((m, LANES), F32),
    )
    out_specs = (
        pl.BlockSpec((tm, PROJ_TN), lambda i, n: (i, jnp.minimum(n, 3))),
        pl.BlockSpec((tm, PROJ_TN), col(4)),
        pl.BlockSpec((None, tm, PROJ_TN), slab(6)),
        pl.BlockSpec((tm, PROJ_TN), col(6)),
        pl.BlockSpec((None, tm, PROJ_TN), slab(8)),
        pl.BlockSpec((tm, PROJ_TN), col(8)),
        pl.BlockSpec((tm, N_HEADS), lambda i, n: (i, 0)),
        pl.BlockSpec((tm, LANES), lambda i, n: (i, 0)),
    )
    in_specs = [
        pl.BlockSpec((tm, D_MODEL), lambda i, n: (i, 0)),
        pl.BlockSpec((1, D_MODEL), lambda i, n: (0, 0)),
        pl.BlockSpec((None, D_MODEL, PROJ_TN), lambda i, n: (layer, 0, n)),
        pl.BlockSpec((None, D_MODEL, LANES), lambda i, n: (layer, 0, D_PROJ // LANES)),
        pl.BlockSpec((1, LANES), lambda i, n: (0, 0)),
        pl.BlockSpec((1, HEAD_DIM), lambda i, n: (0, 0)),
        pl.BlockSpec((1, HEAD_DIM), lambda i, n: (0, 0)),
    ]
    args = [x, g1, w_in_b, w_in_b, bf_pad, qg, kg]
    aliases = {}
    if kv_stack is not None:
        aliases = {len(args): 2, len(args) + 1: 4}
        in_specs += [pl.BlockSpec(memory_space=pl.ANY), pl.BlockSpec(memory_space=pl.ANY)]
        args += list(kv_stack)
    return pl.pallas_call(
        functools.partial(_inproj_kernel, tm=tm, n_aliased=len(aliases)),
        grid=(m // tm, nt),
        in_specs=in_specs, out_specs=out_specs, out_shape=out_shape,
        scratch_shapes=[pltpu.VMEM((tm, D_MODEL), BF16)],
        input_output_aliases=aliases,
        compiler_params=_cparams(("arbitrary", "arbitrary")),
        name="inproj",
    )(*args)


CUM_BLK = 256


def _cumsum_kernel(lf_ref, ccol_ref, crow_ref, *, s):
    r = lax.broadcasted_iota(jnp.int32, (CUM_BLK, CUM_BLK), 0)
    c = lax.broadcasted_iota(jnp.int32, (CUM_BLK, CUM_BLK), 1)
    tri = (c <= r).astype(BF16)
    carry = jnp.zeros((1, LANES), F32)
    for b in range(s // CUM_BLK):
        rows = slice(b * CUM_BLK, (b + 1) * CUM_BLK)
        lf = lf_ref[0, rows, :]
        pieces = _split3(lf)
        cs = jnp.dot(tri, pieces[0], preferred_element_type=F32)
        cs = cs + jnp.dot(tri, pieces[1], preferred_element_type=F32)
        cs = cs + jnp.dot(tri, pieces[2], preferred_element_type=F32)
        cs = cs + carry
        carry = cs[CUM_BLK - 1:CUM_BLK, :]
        ccol_ref[0, rows, :] = cs[:, :N_HEADS]
        crow_ref[0, :, rows] = cs.T[:N_HEADS, :]


def _cumsum(logf_pad, nseq, s):
    lf3 = logf_pad.reshape(nseq, s, LANES)
    return pl.pallas_call(
        functools.partial(_cumsum_kernel, s=s),
        grid=(nseq,),
        in_specs=[pl.BlockSpec((1, s, LANES), lambda b: (b, 0, 0))],
        out_specs=(pl.BlockSpec((1, s, N_HEADS), lambda b: (b, 0, 0)),
                   pl.BlockSpec((1, N_HEADS, s), lambda b: (b, 0, 0))),
        out_shape=(jax.ShapeDtypeStruct((nseq, s, N_HEADS), F32),
                   jax.ShapeDtypeStruct((nseq, N_HEADS, s), F32)),
        compiler_params=_cparams(("arbitrary",)),
        name="logf_cumsum",
    )(lf3)


POOL_HALO = 16
CONV_HALO = 8


def _mix_kernel(u_ref, hb_ref, hc_ref, hx_ref, pw_ref, ps_ref, cw_ref, gp_ref, gc_ref,
                mix_ref, pstate_ref, cstate_ref, ubuf, zbuf, *, ts):
    si = pl.program_id(1)

    @pl.when(si == 0)
    def _():
        ubuf[0:POOL_HALO, :] = jnp.zeros((POOL_HALO, POOL_WIDTH), F32)
        zbuf[0:CONV_HALO, :] = jnp.zeros((CONV_HALO, CONV_WIDTH), F32)

    @pl.when(si > 0)
    def _():
        ubuf[0:POOL_HALO, :] = ubuf[ts:ts + POOL_HALO, :]
        zbuf[0:CONV_HALO, :] = zbuf[ts:ts + CONV_HALO, :]

    u = u_ref[...]
    ubuf[POOL_HALO:POOL_HALO + ts, :] = u
    z = hc_ref[...] * hx_ref[...]
    zbuf[CONV_HALO:CONV_HALO + ts, :] = z

    pos = si * ts + lax.broadcasted_iota(jnp.int32, (ts, 1), 0)
    ys = []
    for g, win in enumerate(POOL_WINDOWS):
        cols = slice(g * POOL_GROUP_DIM, (g + 1) * POOL_GROUP_DIM)
        ug = u[:, cols]
        acc = ug
        for j in range(1, win):
            acc = acc + ubuf[POOL_HALO - j:POOL_HALO - j + ts, cols]
        cnt = jnp.minimum(pos + 1, win).astype(F32)
        pooled = (acc / cnt - ug).astype(BF16)
        ys.append(jnp.dot(pooled, pw_ref[g], preferred_element_type=F32))
    y_pool = jnp.concatenate(ys, axis=1) * ps_ref[...]
    mix_ref[:, 0:POOL_WIDTH] = _rms(y_pool, gp_ref[...]).astype(BF16)

    conv = (cw_ref[0:1, :] * zbuf[CONV_HALO - 2:CONV_HALO - 2 + ts, :]
            + cw_ref[1:2, :] * zbuf[CONV_HALO - 1:CONV_HALO - 1 + ts, :]
            + cw_ref[2:3, :] * z)
    y_conv = hb_ref[...] * conv
    mix_ref[:, POOL_WIDTH:POOL_WIDTH + CONV_WIDTH] = _rms(y_conv, gc_ref[...]).astype(BF16)

    pstate_ref[0] = ubuf[POOL_HALO + ts - POOL_STATE:POOL_HALO + ts, :]
    cstate_ref[0] = zbuf[CONV_HALO + ts - (CONV_K - 1):CONV_HALO + ts, :]


def _mixers(proj, pool_w_b, pool_scale, conv_w, g_pool, g_conv, nseq, s, ts=512):
    m = proj.shape[0]
    nst = s // ts
    pcol = lambda j: pl.BlockSpec((ts, PROJ_TN), lambda b, i: (b * nst + i, j))
    const = lambda shape: pl.BlockSpec(shape, lambda b, i: (0,) * len(shape))
    return pl.pallas_call(
        functools.partial(_mix_kernel, ts=ts),
        grid=(nseq, nst),
        in_specs=[pcol(0), pcol(1), pcol(2), pcol(3),
                  const((len(POOL_WINDOWS), POOL_GROUP_DIM, POOL_GROUP_DIM)),
                  const((1, POOL_WIDTH)), const((CONV_K, CONV_WIDTH)),
                  const((1, POOL_WIDTH)), const((1, CONV_WIDTH))],
        out_specs=(pl.BlockSpec((ts, POOL_WIDTH + CONV_WIDTH), lambda b, i: (b * nst + i, 0)),
                   pl.BlockSpec((1, POOL_STATE, POOL_WIDTH), lambda b, i: (b, 0, 0)),
                   pl.BlockSpec((1, CONV_K - 1, CONV_WIDTH), lambda b, i: (b, 0, 0))),
        out_shape=(jax.ShapeDtypeStruct((m, POOL_WIDTH + CONV_WIDTH), BF16),
                   jax.ShapeDtypeStruct((nseq, POOL_STATE, POOL_WIDTH), F32),
                   jax.ShapeDtypeStruct((nseq, CONV_K - 1, CONV_WIDTH), F32)),
        scratch_shapes=[pltpu.VMEM((POOL_HALO + ts, POOL_WIDTH), F32),
                        pltpu.VMEM((CONV_HALO + ts, CONV_WIDTH), F32)],
        compiler_params=_cparams(("arbitrary", "arbitrary")),
        name="prompt_mixers",
    )(proj, proj, proj, proj, pool_w_b, pool_scale, conv_w, g_pool, g_conv)


def _fox_kernel(q_ref, k_ref, v_ref, cq_ref, ck_ref, g_ref, out_ref, m_scr, l_scr, cq_scr, acc_scr, *, tq, tk):
    i = pl.program_id(1)
    causal = (lax.broadcasted_iota(jnp.int32, (tq, tq), 1) <= lax.broadcasted_iota(jnp.int32, (tq, tq), 0))

    for h in range(N_HEADS):
        m_scr[h] = jnp.full((tq, LANES), -jnp.inf, F32)
        l_scr[h] = jnp.zeros((tq, LANES), F32)
        cq_scr[h] = jnp.broadcast_to(cq_ref[0, :, h:h + 1], (tq, LANES))
    acc_scr[...] = jnp.zeros(acc_scr.shape, F32)

    def step(k0, width, masked):
        ks = pl.ds(pl.multiple_of(k0, width), width)
        lanes = lambda x: jnp.concatenate([x] * (width // LANES), axis=1)
        ones = jnp.ones((width, HEAD_DIM), BF16)
        for h in range(N_HEADS):
            cs = slice(h * HEAD_DIM, (h + 1) * HEAD_DIM)
            s = lax.dot_general(q_ref[:, cs], k_ref[ks, cs], (((1,), (1,)), ((), ())),
                                preferred_element_type=F32)
            s = s + lanes(cq_scr[h]) - ck_ref[0, h:h + 1, ks]
            if masked:
                s = jnp.where(causal, s, -jnp.inf)
            m_prev = m_scr[h]
            m_new = jnp.maximum(m_prev, jnp.broadcast_to(jnp.max(s, axis=1, keepdims=True), (tq, LANES)))
            alpha = jnp.exp(m_prev - m_new)
            p = jnp.exp(s - lanes(m_new)).astype(BF16)
            pv = jnp.dot(p, jnp.concatenate([v_ref[ks, cs], ones], axis=1), preferred_element_type=F32)
            l_scr[h] = alpha * l_scr[h] + pv[:, HEAD_DIM:]
            acc_scr[:, cs] = alpha * acc_scr[:, cs] + pv[:, :HEAD_DIM]
            m_scr[h] = m_new

    def body(j, c):
        step(j * tk, tk, False)
        return c

    n_wide = (i * tq) // tk
    lax.fori_loop(0, n_wide, body, 0)

    @pl.when(n_wide * tk < i * tq)
    def _():
        step(n_wide * tk, tq, False)

    step(i * tq, tq, True)

    for h in range(N_HEADS):
        cs = slice(h * HEAD_DIM, (h + 1) * HEAD_DIM)
        acc_scr[:, cs] = acc_scr[:, cs] / l_scr[h]
    out_ref[...] = _rms(acc_scr[...], g_ref[...]).astype(BF16)


def _fox_prompt(qb, kb, vb, ccol, crow, g_att, nseq, s, tq=256, tk=512):
    assert tk == 2 * tq and s % tk == 0
    m = qb.shape[0]
    nq = s // tq
    stat = pltpu.VMEM((N_HEADS, tq, LANES), F32)
    return pl.pallas_call(
        functools.partial(_fox_kernel, tq=tq, tk=tk),
        grid=(nseq, nq),
        in_specs=[pl.BlockSpec((tq, ATT_WIDTH), lambda b, i: (b * nq + i, 0)),
                  pl.BlockSpec((s, ATT_WIDTH), lambda b, i: (b, 0)),
                  pl.BlockSpec((s, ATT_WIDTH), lambda b, i: (b, 0)),
                  pl.BlockSpec((1, tq, N_HEADS), lambda b, i: (b, i, 0)),
                  pl.BlockSpec((1, N_HEADS, s), lambda b, i: (b, 0, 0)),
                  pl.BlockSpec((1, ATT_WIDTH), lambda b, i: (0, 0))],
        out_specs=pl.BlockSpec((tq, ATT_WIDTH), lambda b, i: (b * nq + i, 0)),
        out_shape=jax.ShapeDtypeStruct((m, ATT_WIDTH), BF16),
        scratch_shapes=[stat, stat, stat, pltpu.VMEM((tq, ATT_WIDTH), F32)],
        compiler_params=_cparams(("arbitrary", "arbitrary")),
        name="fox_prompt",
    )(qb, kb, vb, ccol, crow, g_att)


def _outproj_kernel(x_ref, ma_ref, mb_ref, wa_ref, wb_ref, o_ref):
    acc = jnp.dot(ma_ref[...], wa_ref[...], preferred_element_type=F32)
    acc = acc + jnp.dot(mb_ref[...], wb_ref[...], preferred_element_type=F32)
    o_ref[...] = x_ref[...] + acc


def _outproj(layer, x, mix_a, mix_b, w_o_b, tm, tn=512):
    m = x.shape[0]
    half = D_MODEL // 2
    return pl.pallas_call(
        _outproj_kernel,
        grid=(m // tm, D_MODEL // tn),
        in_specs=[pl.BlockSpec((tm, tn), lambda i, n: (i, n)),
                  pl.BlockSpec((tm, half), lambda i, n: (i, 0)),
                  pl.BlockSpec((tm, half), lambda i, n: (i, 0)),
                  pl.BlockSpec((None, half, tn), lambda i, n: (layer, 0, n)),
                  pl.BlockSpec((None, half, tn), lambda i, n: (layer, 1, n))],
        out_specs=pl.BlockSpec((tm, tn), lambda i, n: (i, n)),
        out_shape=jax.ShapeDtypeStruct((m, D_MODEL), F32),
        compiler_params=_cparams(("arbitrary", "arbitrary")),
        name="outproj",
    )(x, mix_a, mix_b, w_o_b, w_o_b)


FFN_HALO = 8


def _ffn_kernel(*refs, tm, nf, tiles_per_seq, per_row_state):
    if per_row_state:
        (x_ref, g2_ref, wa_ref, wg_ref, cw_ref, cb_ref, wd_ref, prev_ref,
         o_ref, state_ref, xn_ref, act_scr, abuf, carry_ref) = refs
    else:
        (x_ref, g2_ref, wa_ref, wg_ref, cw_ref, cb_ref, wd_ref,
         o_ref, state_ref, xn_ref, act_scr, abuf, carry_ref) = refs
    mi = pl.program_id(0)
    f = pl.program_id(1)
    rc = min(tm, 256)

    @pl.when(f == 0)
    def _():
        def body(i, c):
            r = pl.ds(pl.multiple_of(i * rc, rc), rc)
            x = x_ref[r, :]
            xn_ref[r, :] = _rms(x, g2_ref[...]).astype(BF16)
            o_ref[r, :] = x
            return c
        lax.fori_loop(0, tm // rc, body, 0)

        @pl.when(mi == 0)
        def _():
            carry_ref[...] = jnp.zeros(carry_ref.shape, F32)

    def up():
        xn = xn_ref[...]
        a = jnp.dot(xn, wa_ref[...], preferred_element_type=F32)
        gate = jnp.dot(xn, wg_ref[...], preferred_element_type=F32)
        if per_row_state:
            p1 = prev_ref[1]
            ac = cw_ref[0:1, :] * prev_ref[0] + cw_ref[1:2, :] * p1 + cw_ref[2:3, :] * a
            state_ref[0] = p1
            state_ref[1] = a
        else:
            first = (mi % tiles_per_seq) == 0
            abuf[0:FFN_HALO, :] = jnp.where(first, 0.0, carry_ref[f])
            abuf[FFN_HALO:FFN_HALO + tm, :] = a
            carry_ref[f] = a[tm - FFN_HALO:tm, :]
            ac = (cw_ref[0:1, :] * abuf[FFN_HALO - 2:FFN_HALO - 2 + tm, :]
                  + cw_ref[1:2, :] * abuf[FFN_HALO - 1:FFN_HALO - 1 + tm, :]
                  + cw_ref[2:3, :] * a)
            state_ref[0] = a[tm - (FFN_K - 1):tm, :]
        ac = ac + cb_ref[...]
        act_scr[f % 2] = (ac * jax.nn.sigmoid(ac) * gate).astype(BF16)

    def down():
        o_ref[...] += jnp.dot(act_scr[(f + 1) % 2], wd_ref[...], preferred_element_type=F32)

    @pl.when(f == 0)
    def _():
        up()

    @pl.when((f > 0) & (f < nf))
    def _():
        down()
        up()

    @pl.when(f == nf)
    def _():
        down()


def _ffn(layer, x, g2, w_up_b, cw, cb, w_down_b, prev, tm, seq_len, tf=512):
    m = x.shape[0]
    nf = D_FF // tf
    per_row_state = prev is not None
    cur = lambda f: jnp.minimum(f, nf - 1)
    old = lambda f: jnp.maximum(f - 1, 0)
    in_specs = [pl.BlockSpec((tm, D_MODEL), lambda i, f: (i, 0), pipeline_mode=pl.Buffered(1)),
                pl.BlockSpec((1, D_MODEL), lambda i, f: (0, 0)),
                pl.BlockSpec((None, D_MODEL, tf), lambda i, f: (layer, 0, cur(f))),
                pl.BlockSpec((None, D_MODEL, tf), lambda i, f: (layer, 0, nf + cur(f))),
                pl.BlockSpec((FFN_K, tf), lambda i, f: (0, cur(f))),
                pl.BlockSpec((1, tf), lambda i, f: (0, cur(f))),
                pl.BlockSpec((None, tf, D_MODEL), lambda i, f: (layer, old(f), 0))]
    args = [x, g2, w_up_b, w_up_b, cw, cb, w_down_b]
    if per_row_state:
        tiles_per_seq = 1
        in_specs.append(pl.BlockSpec((FFN_K - 1, tm, tf), lambda i, f: (0, i, cur(f))))
        args.append(prev)
        state_spec = pl.BlockSpec((FFN_K - 1, tm, tf), lambda i, f: (0, i, cur(f)))
        state_shape = jax.ShapeDtypeStruct((FFN_K - 1, m, D_FF), F32)
    else:
        tiles_per_seq = seq_len // tm
        state_spec = pl.BlockSpec((1, FFN_K - 1, tf), lambda i, f: (i, 0, cur(f)))
        state_shape = jax.ShapeDtypeStruct((m // tm, FFN_K - 1, D_FF), F32)
    x_out, state = pl.pallas_call(
        functools.partial(_ffn_kernel, tm=tm, nf=nf, tiles_per_seq=tiles_per_seq, per_row_state=per_row_state),
        grid=(m // tm, nf + 1),
        in_specs=in_specs,
        out_specs=(pl.BlockSpec((tm, D_MODEL), lambda i, f: (i, 0)), state_spec),
        out_shape=(jax.ShapeDtypeStruct((m, D_MODEL), F32), state_shape),
        scratch_shapes=[pltpu.VMEM((tm, D_MODEL), BF16),
                        pltpu.VMEM((2, tm, tf), BF16),
                        pltpu.VMEM((FFN_HALO + tm, tf), F32),
                        pltpu.VMEM((nf, FFN_HALO, tf), F32)],
        compiler_params=_cparams(("arbitrary", "arbitrary")),
        name="ffn",
    )(*args)
    if not per_row_state:
        state = state[tiles_per_seq - 1::tiles_per_seq]
    return x_out, state


PAGE_ROW = PAGE_SIZE * N_HEADS


def _bias_kernel(pt_ref, lf_hbm, lfnew_ref, bias_ref, lfbuf, sem, *, db, n_pages):
    n_rows = db * n_pages

    def copy(r):
        return pltpu.make_async_copy(lf_hbm.at[pl.ds(pt_ref[r], 1)], lfbuf.at[pl.ds(r, 1)], sem)

    def start(r, c):
        copy(r).start()
        return c

    def wait(r, c):
        copy(r).wait()
        return c

    lax.fori_loop(0, n_rows, start, 0)
    lax.fori_loop(0, n_rows, wait, 0)

    src = lax.broadcasted_iota(jnp.int32, (PAGE_ROW, PAGE_ROW), 0)
    dst = lax.broadcasted_iota(jnp.int32, (PAGE_ROW, PAGE_ROW), 1)
    same_head = (src & (N_HEADS - 1)) == (dst & (N_HEADS - 1))
    later = (src >> 3) > (dst >> 3)
    m_suffix = (same_head & later).astype(BF16)
    m_total = same_head.astype(BF16)
    pr = lax.broadcasted_iota(jnp.int32, (n_pages, n_pages), 0)
    pc = lax.broadcasted_iota(jnp.int32, (n_pages, n_pages), 1)
    later_page = (pc > pr).astype(BF16)

    pieces = _split3(lfbuf[...])
    within = _dot01(pieces, m_suffix)
    total = _dot01(pieces, m_total)
    for b in range(db):
        rows = slice(b * n_pages, (b + 1) * n_pages)
        t0, t1, t2 = _split3(total[rows])
        after = jnp.dot(later_page, t0, preferred_element_type=F32)
        after = after + jnp.dot(later_page, t1, preferred_element_type=F32)
        after = after + jnp.dot(later_page, t2, preferred_element_type=F32)
        bias_ref[b] = within[rows] + after + lfnew_ref[b]


def _sample_bias(page_table, logf_flat, lfnew_rep):
    db, n_pages = page_table.shape
    grid_spec = pltpu.PrefetchScalarGridSpec(
        num_scalar_prefetch=1,
        grid=(1,),
        in_specs=[pl.BlockSpec(memory_space=pl.ANY),
                  pl.BlockSpec((db, 1, PAGE_ROW), lambda i, pt: (0, 0, 0))],
        out_specs=pl.BlockSpec((db, n_pages, PAGE_ROW), lambda i, pt: (0, 0, 0)),
        scratch_shapes=[pltpu.VMEM((db * n_pages, PAGE_ROW), F32), pltpu.SemaphoreType.DMA(())],
    )
    return pl.pallas_call(
        functools.partial(_bias_kernel, db=db, n_pages=n_pages),
        grid_spec=grid_spec,
        out_shape=jax.ShapeDtypeStruct((db, n_pages, PAGE_ROW), F32),
        compiler_params=_cparams(("arbitrary",)),
        name="sample_bias",
    )(page_table.reshape(db * n_pages), logf_flat, lfnew_rep)


def _head_allreduce(x, op, reduce_op):
    x = jnp.broadcast_to(reduce_op(x, axis=0, keepdims=True), (SUBLANES, LANES))
    for sh in (8, 16, 32, 64):
        x = op(x, pltpu.roll(x, sh, axis=1))
    return x


def _paged_kernel(pt_ref, q_ref, bias_ref, knew_ref, vnew_ref, *refs, n_steps):
    k_refs = refs[:PAGES_PER_STEP]
    v_refs = refs[PAGES_PER_STEP:2 * PAGES_PER_STEP]
    o_ref = refs[2 * PAGES_PER_STEP]
    m_scr, l_scr, acc_scr, s_scr = refs[2 * PAGES_PER_STEP + 1:]
    i = pl.program_id(1)

    @pl.when(i == 0)
    def _():
        m_scr[...] = jnp.full((SUBLANES, LANES), -jnp.inf, F32)
        l_scr[...] = jnp.zeros((SUBLANES, LANES), F32)
        acc_scr[...] = jnp.zeros((N_HEADS, HEAD_DIM), F32)

    q = q_ref[0]
    sub = lax.broadcasted_iota(jnp.int32, (N_HEADS, PAGE_ROW), 0)
    lane = lax.broadcasted_iota(jnp.int32, (N_HEADS, PAGE_ROW), 1)
    own_head = sub == (lane & (N_HEADS - 1))

    for j in range(PAGES_PER_STEP):
        k2 = k_refs[j][...].reshape(PAGE_ROW, HEAD_DIM).astype(BF16)
        st = lax.dot_general(q, k2, (((1,), (1,)), ((), ())), preferred_element_type=F32)
        s_scr[j:j + 1, :] = jnp.sum(jnp.where(own_head, st, 0.0), axis=0, keepdims=True)

    s = s_scr[...] + bias_ref[0]
    chunks = [s[:, c * LANES:(c + 1) * LANES] for c in range(PAGE_ROW // LANES)]
    mx = chunks[0]
    for c in chunks[1:]:
        mx = jnp.maximum(mx, c)
    m_prev = m_scr[...]
    m_new = jnp.maximum(m_prev, _head_allreduce(mx, jnp.maximum, jnp.max))
    alpha = jnp.exp(m_prev - m_new)
    p = jnp.exp(s - jnp.concatenate([m_new[0:1, :]] * (PAGE_ROW // LANES), axis=1))
    ps = p[:, 0:LANES]
    for c in range(1, PAGE_ROW // LANES):
        ps = ps + p[:, c * LANES:(c + 1) * LANES]
    l_scr[...] = alpha * l_scr[...] + _head_allreduce(ps, jnp.add, jnp.sum)
    m_scr[...] = m_new

    o = jnp.zeros((N_HEADS, HEAD_DIM), F32)
    for j in range(PAGES_PER_STEP):
        pj = jnp.where(own_head, jnp.broadcast_to(p[j:j + 1, :], (N_HEADS, PAGE_ROW)), 0.0).astype(BF16)
        v2 = v_refs[j][...].reshape(PAGE_ROW, HEAD_DIM).astype(BF16)
        o = o + jnp.dot(pj, v2, preferred_element_type=F32)

    sub8 = lax.broadcasted_iota(jnp.int32, (SUBLANES, LANES), 0)
    lane8 = lax.broadcasted_iota(jnp.int32, (SUBLANES, LANES), 1)
    diag = sub8 == lane8

    def to_col(x):
        return jnp.sum(jnp.where(diag, x, 0.0), axis=1, keepdims=True)

    acc = to_col(alpha) * acc_scr[...] + o
    acc_scr[...] = acc

    @pl.when(i == n_steps - 1)
    def _():
        m_col = to_col(m_new)
        l_col = to_col(l_scr[...])
        qf = q.astype(F32)
        kn = knew_ref[0].astype(BF16).astype(F32)
        s_new = jnp.sum(qf * kn, axis=1, keepdims=True)
        m_f = jnp.maximum(m_col, s_new)
        a_old = jnp.exp(m_col - m_f)
        p_new = jnp.exp(s_new - m_f)
        o_ref[0] = (acc * a_old + p_new * vnew_ref[0]) / (l_col * a_old + p_new)


def _paged_attention(layer, page_table, q3, bias, k_new3, v_new3, cache_k, cache_v):
    db, n_pages = page_table.shape
    n_steps = n_pages // PAGES_PER_STEP

    def page_spec(j):
        return pl.BlockSpec((None, None, PAGE_SIZE, N_HEADS, HEAD_DIM),
                            lambda b, i, pt: (layer, pt[b, i * PAGES_PER_STEP + j], 0, 0, 0))

    row = pl.BlockSpec((1, N_HEADS, HEAD_DIM), lambda b, i, pt: (b, 0, 0))
    grid_spec = pltpu.PrefetchScalarGridSpec(
        num_scalar_prefetch=1,
        grid=(db, n_steps),
        in_specs=[row,
                  pl.BlockSpec((1, PAGES_PER_STEP, PAGE_ROW), lambda b, i, pt: (b, i, 0)),
                  row, row]
                 + [page_spec(j) for j in range(PAGES_PER_STEP)]
                 + [page_spec(j) for j in range(PAGES_PER_STEP)],
        out_specs=row,
        scratch_shapes=[pltpu.VMEM((SUBLANES, LANES), F32), pltpu.VMEM((SUBLANES, LANES), F32),
                        pltpu.VMEM((N_HEADS, HEAD_DIM), F32), pltpu.VMEM((PAGES_PER_STEP, PAGE_ROW), F32)],
    )
    return pl.pallas_call(
        functools.partial(_paged_kernel, n_steps=n_steps),
        grid_spec=grid_spec,
        out_shape=jax.ShapeDtypeStruct((db, N_HEADS, HEAD_DIM), F32),
        compiler_params=_cparams(("arbitrary", "arbitrary")),
        name="paged_attention",
    )(page_table, q3, bias, k_new3, v_new3, *([cache_k] * PAGES_PER_STEP), *([cache_v] * PAGES_PER_STEP))


def _sample_mix_kernel(proj_ref, sp_ref, sc_ref, yatt_ref, pw_ref, ps_ref, cw_ref, gp_ref, gc_ref, ga_ref,
                       mixa_ref, mixb_ref, pstate_ref, cstate_ref):
    u = proj_ref[:, 0:POOL_WIDTH]
    hb = proj_ref[:, POOL_WIDTH:POOL_WIDTH + CONV_WIDTH]
    hc = proj_ref[:, POOL_WIDTH + CONV_WIDTH:POOL_WIDTH + 2 * CONV_WIDTH]
    hx = proj_ref[:, POOL_WIDTH + 2 * CONV_WIDTH:POOL_WIDTH + 3 * CONV_WIDTH]

    ys = []
    for g, win in enumerate(POOL_WINDOWS):
        cols = slice(g * POOL_GROUP_DIM, (g + 1) * POOL_GROUP_DIM)
        ug = u[:, cols]
        acc = ug
        for j in range(1, win):
            acc = acc + sp_ref[POOL_STATE - j, :, cols]
        cnt = float(min(POOL_STATE + 1, win))
        pooled = (acc / cnt - ug).astype(BF16)
        ys.append(jnp.dot(pooled, pw_ref[g], preferred_element_type=F32))
    y_pool = jnp.concatenate(ys, axis=1) * ps_ref[...]
    mixa_ref[:, 0:POOL_WIDTH] = _rms(y_pool, gp_ref[...]).astype(BF16)

    z = hc * hx
    conv = cw_ref[0:1, :] * sc_ref[0] + cw_ref[1:2, :] * sc_ref[1] + cw_ref[2:3, :] * z
    mixa_ref[:, POOL_WIDTH:POOL_WIDTH + CONV_WIDTH] = _rms(hb * conv, gc_ref[...]).astype(BF16)
    mixb_ref[...] = _rms(yatt_ref[...], ga_ref[...]).astype(BF16)

    for r in range(POOL_STATE - 1):
        pstate_ref[r] = sp_ref[r + 1]
    pstate_ref[POOL_STATE - 1] = u
    cstate_ref[0] = sc_ref[1]
    cstate_ref[1] = z


def _sample_mixers(proj, sp_t, sc_t, y_att, pool_w_b, pool_scale, conv_w, g_pool, g_conv, g_att):
    db = proj.shape[0]
    return pl.pallas_call(
        _sample_mix_kernel,
        out_shape=(jax.ShapeDtypeStruct((db, POOL_WIDTH + CONV_WIDTH), BF16),
                   jax.ShapeDtypeStruct((db, ATT_WIDTH), BF16),
                   jax.ShapeDtypeStruct((POOL_STATE, db, POOL_WIDTH), F32),
                   jax.ShapeDtypeStruct((CONV_K - 1, db, CONV_WIDTH), F32)),
        compiler_params=pltpu.CompilerParams(vmem_limit_bytes=VMEM_LIMIT),
        name="sample_mixers",
    )(proj, sp_t, sc_t, y_att, pool_w_b, pool_scale, conv_w, g_pool, g_conv, g_att)


def kernel(x_prompt, x_sample, cache_k, cache_v, cache_logf, state_pool, state_conv, state_ffn, page_table,
           norm1_g, w_in, b_f, pool_w, pool_scale, conv_w, q_norm_g, k_norm_g, out_norm_g, w_o, norm2_g,
           w_up, ffn_conv_w, ffn_conv_b, w_down):
    nb, s, _ = x_prompt.shape
    db = x_sample.shape[0]
    depth = w_in.shape[0]
    n_pool = cache_k.shape[1]

    xp = x_prompt.reshape(nb * s, D_MODEL)
    xs = x_sample.reshape(db, D_MODEL)
    logf_flat = cache_logf.reshape(depth, n_pool, PAGE_ROW)

    w_in_b = jnp.pad(w_in, ((0, 0), (0, 0), (0, LANES - N_HEADS))).astype(BF16)
    w_o_b = w_o.astype(BF16)
    w_up_b = w_up.astype(BF16)
    w_down_b = w_down.astype(BF16)

    outs = {name: [] for name in ("fp", "pp", "cp", "ffp", "fs", "ps", "cs", "ffs")}
    kv_prompt = (jnp.zeros((depth, nb * s, ATT_WIDTH), F32), jnp.ones((depth, nb * s, ATT_WIDTH), F32))
    kv_sample = (jnp.zeros((depth, db, ATT_WIDTH), F32), jnp.ones((depth, db, ATT_WIDTH), F32))
    for l in range(depth):
        bf_pad = jnp.pad(b_f[l], (0, LANES - N_HEADS)).reshape(1, LANES)
        pool_w_b = pool_w[l].astype(BF16)
        g1 = norm1_g[l].reshape(1, D_MODEL)
        g2 = norm2_g[l].reshape(1, D_MODEL)
        qg = q_norm_g[l].reshape(1, HEAD_DIM)
        kg = k_norm_g[l].reshape(1, HEAD_DIM)
        ps_l = pool_scale[l].reshape(1, POOL_WIDTH)
        g_pool = out_norm_g[l, :POOL_WIDTH].reshape(1, POOL_WIDTH)
        g_conv = out_norm_g[l, POOL_WIDTH:POOL_WIDTH + CONV_WIDTH].reshape(1, CONV_WIDTH)
        g_att = out_norm_g[l, POOL_WIDTH + CONV_WIDTH:].reshape(1, ATT_WIDTH)
        cb = ffn_conv_b[l].reshape(1, D_FF)

        proj, qb, kp_stack, kb, vp_stack, vb, logf, logf_pad = _inproj(
            l, xp, g1, w_in_b, bf_pad, qg, kg, tm=1024, kv_stack=kv_prompt)
        kv_prompt = (kp_stack, vp_stack)
        ccol, crow = _cumsum(logf_pad, nb, s)
        mix_a, pstate, cstate = _mixers(proj, pool_w_b, ps_l, conv_w[l], g_pool, g_conv, nb, s)
        mix_b = _fox_prompt(qb, kb, vb, ccol, crow, g_att, nb, s)
        x_mid = _outproj(l, xp, mix_a, mix_b, w_o_b, tm=1024)
        xp, fstate = _ffn(l, x_mid, g2, w_up_b, ffn_conv_w[l], cb, w_down_b, None, tm=PROMPT_FFN_TM, seq_len=s)
        outs["fp"].append(logf.reshape(nb, s, N_HEADS))
        outs["pp"].append(pstate)
        outs["cp"].append(cstate)
        outs["ffp"].append(fstate)

        proj_s, qb_s, ks_stack, _, vs_stack, _, logf_s, _ = _inproj(
            l, xs, g1, w_in_b, bf_pad, qg, kg, tm=db, kv_stack=kv_sample)
        kv_sample = (ks_stack, vs_stack)
        k32_s, v32_s = ks_stack[l], vs_stack[l]
        lfnew_rep = jnp.tile(logf_s, (1, PAGE_SIZE)).reshape(db, 1, PAGE_ROW)
        bias = _sample_bias(page_table, logf_flat[l], lfnew_rep)
        y_att = _paged_attention(l, page_table, qb_s.reshape(db, N_HEADS, HEAD_DIM), bias,
                                 k32_s.reshape(db, N_HEADS, HEAD_DIM), v32_s.reshape(db, N_HEADS, HEAD_DIM),
                                 cache_k, cache_v)
        mix_a_s, mix_b_s, pstate_t, cstate_t = _sample_mixers(
            proj_s, jnp.swapaxes(state_pool[l], 0, 1), jnp.swapaxes(state_conv[l], 0, 1),
            y_att.reshape(db, ATT_WIDTH), pool_w_b, ps_l, conv_w[l], g_pool, g_conv, g_att)
        xs_mid = _outproj(l, xs, mix_a_s, mix_b_s, w_o_b, tm=db)
        xs, fstate_t = _ffn(l, xs_mid, g2, w_up_b, ffn_conv_w[l], cb, w_down_b,
                            jnp.swapaxes(state_ffn[l], 0, 1), tm=db, seq_len=1)
        outs["fs"].append(logf_s.reshape(db, 1, N_HEADS))
        outs["ps"].append(jnp.swapaxes(pstate_t, 0, 1))
        outs["cs"].append(jnp.swapaxes(cstate_t, 0, 1))
        outs["ffs"].append(jnp.swapaxes(fstate_t, 0, 1))

    st = lambda name: jnp.stack(outs[name])
    return (xp.reshape(nb, s, D_MODEL), xs.reshape(db, 1, D_MODEL),
            kv_prompt[0].reshape(depth, nb, s, N_HEADS, HEAD_DIM), kv_prompt[1].reshape(depth, nb, s, N_HEADS, HEAD_DIM),
            st("fp"), st("pp"), st("cp"), st("ffp"),
            kv_sample[0].reshape(depth, db, 1, N_HEADS, HEAD_DIM), kv_sample[1].reshape(depth, db, 1, N_HEADS, HEAD_DIM),
            st("fs"), st("ps"), st("cs"), st("ffs"))
```

```python
import functools
from typing import NamedTuple

import jax
import jax.numpy as jnp
from jax import lax
from jax.experimental import pallas as pl
from jax.experimental.pallas import tpu as pltpu

F32 = jnp.float32
BF16 = jnp.bfloat16

D_MODEL = 2048
PAGE_SIZE = 128
POOL_WIDTH = 512
POOL_WINDOWS = (2, 4, 8, 16)
POOL_GROUP_DIM = 128
POOL_STATE = 15
CONV_WIDTH = 512
CONV_K = 3
ATT_WIDTH = 1024
HEAD_DIM = 128
N_HEADS = 8
D_FF = 5632
FFN_K = 3
EPS = 1e-6
D_PROJ = POOL_WIDTH + 3 * CONV_WIDTH + 3 * ATT_WIDTH
SCALE = HEAD_DIM ** -0.5

LANES = 128
SUBLANES = 8
VMEM_LIMIT = 56 * 1024 * 1024

PROJ_TN = 512
PROMPT_FFN_TM = 1024
FOX_TQ = 256
IN_CAST_STEPS = 64

T_Q = (POOL_WIDTH + 3 * CONV_WIDTH) // PROJ_TN
T_K = T_Q + ATT_WIDTH // PROJ_TN
T_V = T_K + ATT_WIDTH // PROJ_TN
T_END = D_PROJ // PROJ_TN
PAGES_PER_STEP = 16


def _cparams(sem):
    return pltpu.CompilerParams(dimension_semantics=sem, vmem_limit_bytes=VMEM_LIMIT)


def _rms(x, g):
    ms = jnp.mean(x * x, axis=-1, keepdims=True)
    return x * lax.rsqrt(ms + EPS) * g


def _log_sigmoid(z):
    return jnp.minimum(z, 0.0) - jnp.log1p(jnp.exp(-jnp.abs(z)))


def _split3(x):
    p0 = x.astype(BF16)
    r = x - p0.astype(F32)
    p1 = r.astype(BF16)
    p2 = (r - p1.astype(F32)).astype(BF16)
    return p0, p1, p2


def _dot01(a_pieces, b):
    out = jnp.dot(a_pieces[0], b, preferred_element_type=F32)
    for p in a_pieces[1:]:
        out = out + jnp.dot(p, b, preferred_element_type=F32)
    return out


class _CastJob(NamedTuple):
    src: jax.Array
    layer: int
    chunk_rows: int
    out_cols: int


def _cast_plumbing(jobs, step_of):
    in_specs, out_specs, out_shapes = [], [], []
    for job in jobs:
        _, rows, cols = job.src.shape
        n_chunks = rows // job.chunk_rows
        chunk = lambda *g, n_chunks=n_chunks: jnp.minimum(step_of(*g), n_chunks - 1)
        in_specs.append(pl.BlockSpec((None, job.chunk_rows, cols),
                                     lambda *g, job=job, chunk=chunk: (job.layer, chunk(*g), 0)))
        out_specs.append(pl.BlockSpec((job.chunk_rows, job.out_cols), lambda *g, chunk=chunk: (chunk(*g), 0)))
        out_shapes.append(jax.ShapeDtypeStruct((rows, job.out_cols), BF16))
    return in_specs, out_specs, out_shapes


def _run_casts(src_refs, dst_refs):
    for src, dst in zip(src_refs, dst_refs):
        cols = src.shape[1]
        if dst.shape[1] > cols:
            dst[...] = jnp.zeros(dst.shape, BF16)
        dst[:, :cols] = src[...].astype(BF16)


def _inproj_kernel(x_ref, g1_ref, w_ref, wf_ref, bf_ref, qg_ref, kg_ref, *rest, tm, n_aliased, n_casts, cast_steps):
    cast_src = rest[n_aliased:n_aliased + n_casts]
    proj_ref, q_ref, k32_ref, kb_ref, v32_ref, vb_ref, logf_ref, logfp_ref = rest[n_aliased + n_casts:][:8]
    cast_dst = rest[n_aliased + n_casts + 8:][:n_casts]
    xn_ref, y_scr = rest[-2:]
    n = pl.program_id(1)
    rc = min(tm, 256)

    if n_casts:
        @pl.when(pl.program_id(0) * pl.num_programs(1) + n < cast_steps)
        def _():
            _run_casts(cast_src, cast_dst)

    def normalise():
        def body(i, c):
            r = pl.ds(pl.multiple_of(i * rc, rc), rc)
            xn_ref[r, :] = _rms(x_ref[r, :], g1_ref[...]).astype(BF16)
            return c
        lax.fori_loop(0, tm // rc, body, 0)

    def matmul():
        return jnp.dot(xn_ref[...], w_ref[...], preferred_element_type=F32)

    def park():
        y_scr[n % 2] = matmul()

    def head_norm(g):
        y = y_scr[(n + 1) % 2]
        parts = []
        for h in range(PROJ_TN // HEAD_DIM):
            parts.append(_rms(y[:, h * HEAD_DIM:(h + 1) * HEAD_DIM], g))
        return jnp.concatenate(parts, axis=1)

    def finish_q():
        q_ref[...] = (head_norm(qg_ref[...]) * SCALE).astype(BF16)

    def finish_k():
        kn = head_norm(kg_ref[...])
        k32_ref[...] = kn
        kb_ref[...] = kn.astype(BF16)

    def direct_v():
        y = matmul()
        v32_ref[...] = y
        vb_ref[...] = y.astype(BF16)

    @pl.when(n == 0)
    def _():
        normalise()
        fl = jnp.dot(xn_ref[...], wf_ref[...], preferred_element_type=F32)
        lf = _log_sigmoid(fl + bf_ref[...])
        lane = lax.broadcasted_iota(jnp.int32, lf.shape, 1)
        lf = jnp.where(lane < N_HEADS, lf, 0.0)
        logfp_ref[...] = lf
        logf_ref[...] = lf[:, :N_HEADS]

    @pl.when(n < T_Q)
    def _():
        proj_ref[...] = matmul()

    @pl.when(n == T_Q)
    def _():
        park()

    @pl.when((n > T_Q) & (n <= T_K))
    def _():
        finish_q()
        park()

    @pl.when((n > T_K) & (n < T_V))
    def _():
        finish_k()
        park()

    @pl.when(n == T_V)
    def _():
        finish_k()
        direct_v()

    @pl.when(n > T_V)
    def _():
        direct_v()


def _inproj(layer, x, g1, w_in_b, bf_pad, qg, kg, tm, kv_stack, cast_jobs=()):
    m = x.shape[0]
    depth = kv_stack[0].shape[0]
    n_steps = T_END
    col = lambda first_step: (lambda i, n: (i, jnp.clip(n - first_step, 0, 1)))
    slab = lambda first_step: (lambda i, n: (layer, i, jnp.clip(n - first_step, 0, 1)))
    out_shape = (
        jax.ShapeDtypeStruct((m, 4 * PROJ_TN), F32),
        jax.ShapeDtypeStruct((m, ATT_WIDTH), BF16),
        jax.ShapeDtypeStruct((depth, m, ATT_WIDTH), F32),
        jax.ShapeDtypeStruct((m, ATT_WIDTH), BF16),
        jax.ShapeDtypeStruct((depth, m, ATT_WIDTH), F32),
        jax.ShapeDtypeStruct((m, ATT_WIDTH), BF16),
        jax.ShapeDtypeStruct((m, N_HEADS), F32),
        jax.ShapeDtypeStruct((m, LANES), F32),
    )
    out_specs = (
        pl.BlockSpec((tm, PROJ_TN), lambda i, n: (i, jnp.minimum(n, T_Q - 1))),
        pl.BlockSpec((tm, PROJ_TN), col(T_Q + 1)),
        pl.BlockSpec((None, tm, PROJ_TN), slab(T_K + 1)),
        pl.BlockSpec((tm, PROJ_TN), col(T_K + 1)),
        pl.BlockSpec((None, tm, PROJ_TN), slab(T_V)),
        pl.BlockSpec((tm, PROJ_TN), col(T_V)),
        pl.BlockSpec((tm, N_HEADS), lambda i, n: (i, 0)),
        pl.BlockSpec((tm, LANES), lambda i, n: (i, 0)),
    )
    in_specs = [
        pl.BlockSpec((tm, D_MODEL), lambda i, n: (i, 0)),
        pl.BlockSpec((1, D_MODEL), lambda i, n: (0, 0)),
        pl.BlockSpec((D_MODEL, PROJ_TN), lambda i, n: (0, n)),
        pl.BlockSpec((D_MODEL, LANES), lambda i, n: (0, D_PROJ // LANES)),
        pl.BlockSpec((1, LANES), lambda i, n: (0, 0)),
        pl.BlockSpec((1, HEAD_DIM), lambda i, n: (0, 0)),
        pl.BlockSpec((1, HEAD_DIM), lambda i, n: (0, 0)),
        pl.BlockSpec(memory_space=pl.ANY), pl.BlockSpec(memory_space=pl.ANY),
    ]
    args = [x, g1, w_in_b, w_in_b, bf_pad, qg, kg, *kv_stack]
    aliases = {7: 2, 8: 4}
    cast_in, cast_out, cast_shapes = _cast_plumbing(cast_jobs, lambda i, n: i * n_steps + n)
    cast_steps = max([j.src.shape[1] // j.chunk_rows for j in cast_jobs], default=0)
    assert cast_steps <= (m // tm) * n_steps
    return pl.pallas_call(
        functools.partial(_inproj_kernel, tm=tm, n_aliased=len(aliases), n_casts=len(cast_jobs),
                          cast_steps=cast_steps),
        grid=(m // tm, n_steps),
        in_specs=in_specs + cast_in, out_specs=out_specs + tuple(cast_out), out_shape=out_shape + tuple(cast_shapes),
        scratch_shapes=[pltpu.VMEM((tm, D_MODEL), BF16), pltpu.VMEM((2, tm, PROJ_TN), F32)],
        input_output_aliases=aliases,
        compiler_params=_cparams(("arbitrary", "arbitrary")),
        name="inproj",
    )(*args, *[j.src for j in cast_jobs])


CUM_BLK = 256


def _cumsum_kernel(lf_ref, ccol_ref, crow_ref, *, s):
    r = lax.broadcasted_iota(jnp.int32, (CUM_BLK, CUM_BLK), 0)
    c = lax.broadcasted_iota(jnp.int32, (CUM_BLK, CUM_BLK), 1)
    tri = (c <= r).astype(BF16)
    carry = jnp.zeros((1, LANES), F32)
    for b in range(s // CUM_BLK):
        rows = slice(b * CUM_BLK, (b + 1) * CUM_BLK)
        lf = lf_ref[0, rows, :]
        pieces = _split3(lf)
        cs = jnp.dot(tri, pieces[0], preferred_element_type=F32)
        cs = cs + jnp.dot(tri, pieces[1], preferred_element_type=F32)
        cs = cs + jnp.dot(tri, pieces[2], preferred_element_type=F32)
        cs = cs + carry
        carry = cs[CUM_BLK - 1:CUM_BLK, :]
        ccol_ref[0, rows, :] = cs[:, :N_HEADS]
        crow_ref[0, :, rows] = cs.T[:N_HEADS, :]


def _cumsum(logf_pad, nseq, s):
    lf3 = logf_pad.reshape(nseq, s, LANES)
    return pl.pallas_call(
        functools.partial(_cumsum_kernel, s=s),
        grid=(nseq,),
        in_specs=[pl.BlockSpec((1, s, LANES), lambda b: (b, 0, 0))],
        out_specs=(pl.BlockSpec((1, s, N_HEADS), lambda b: (b, 0, 0)),
                   pl.BlockSpec((1, N_HEADS, s), lambda b: (b, 0, 0))),
        out_shape=(jax.ShapeDtypeStruct((nseq, s, N_HEADS), F32),
                   jax.ShapeDtypeStruct((nseq, N_HEADS, s), F32)),
        compiler_params=_cparams(("arbitrary",)),
        name="logf_cumsum",
    )(lf3)


POOL_HALO = 16
CONV_HALO = 8


def _mix_kernel(u_ref, hb_ref, hc_ref, hx_ref, pw_ref, ps_ref, cw_ref, gp_ref, gc_ref,
                mix_ref, pstate_ref, cstate_ref, ubuf, zbuf, *, ts):
    si = pl.program_id(1)

    @pl.when(si == 0)
    def _():
        ubuf[0:POOL_HALO, :] = jnp.zeros((POOL_HALO, POOL_WIDTH), F32)
        zbuf[0:CONV_HALO, :] = jnp.zeros((CONV_HALO, CONV_WIDTH), F32)

    @pl.when(si > 0)
    def _():
        ubuf[0:POOL_HALO, :] = ubuf[ts:ts + POOL_HALO, :]
        zbuf[0:CONV_HALO, :] = zbuf[ts:ts + CONV_HALO, :]

    u = u_ref[...]
    ubuf[POOL_HALO:POOL_HALO + ts, :] = u
    z = hc_ref[...] * hx_ref[...]
    zbuf[CONV_HALO:CONV_HALO + ts, :] = z

    pos = si * ts + lax.broadcasted_iota(jnp.int32, (ts, 1), 0)
    ys = []
    for g, win in enumerate(POOL_WINDOWS):
        cols = slice(g * POOL_GROUP_DIM, (g + 1) * POOL_GROUP_DIM)
        ug = u[:, cols]
        acc = ug
        for j in range(1, win):
            acc = acc + ubuf[POOL_HALO - j:POOL_HALO - j + ts, cols]
        cnt = jnp.minimum(pos + 1, win).astype(F32)
        pooled = (acc / cnt - ug).astype(BF16)
        ys.append(jnp.dot(pooled, pw_ref[g], preferred_element_type=F32))
    y_pool = jnp.concatenate(ys, axis=1) * ps_ref[...]
    mix_ref[:, 0:POOL_WIDTH] = _rms(y_pool, gp_ref[...]).astype(BF16)

    conv = (cw_ref[0:1, :] * zbuf[CONV_HALO - 2:CONV_HALO - 2 + ts, :]
            + cw_ref[1:2, :] * zbuf[CONV_HALO - 1:CONV_HALO - 1 + ts, :]
            + cw_ref[2:3, :] * z)
    y_conv = hb_ref[...] * conv
    mix_ref[:, POOL_WIDTH:POOL_WIDTH + CONV_WIDTH] = _rms(y_conv, gc_ref[...]).astype(BF16)

    pstate_ref[0] = ubuf[POOL_HALO + ts - POOL_STATE:POOL_HALO + ts, :]
    cstate_ref[0] = zbuf[CONV_HALO + ts - (CONV_K - 1):CONV_HALO + ts, :]


def _mixers(proj, pool_w_b, pool_scale, conv_w, g_pool, g_conv, nseq, s, ts=512):
    m = proj.shape[0]
    nst = s // ts
    pcol = lambda j: pl.BlockSpec((ts, PROJ_TN), lambda b, i: (b * nst + i, j))
    const = lambda shape: pl.BlockSpec(shape, lambda b, i: (0,) * len(shape))
    return pl.pallas_call(
        functools.partial(_mix_kernel, ts=ts),
        grid=(nseq, nst),
        in_specs=[pcol(0), pcol(1), pcol(2), pcol(3),
                  const((len(POOL_WINDOWS), POOL_GROUP_DIM, POOL_GROUP_DIM)),
                  const((1, POOL_WIDTH)), const((CONV_K, CONV_WIDTH)),
                  const((1, POOL_WIDTH)), const((1, CONV_WIDTH))],
        out_specs=(pl.BlockSpec((ts, POOL_WIDTH + CONV_WIDTH), lambda b, i: (b * nst + i, 0)),
                   pl.BlockSpec((1, POOL_STATE, POOL_WIDTH), lambda b, i: (b, 0, 0)),
                   pl.BlockSpec((1, CONV_K - 1, CONV_WIDTH), lambda b, i: (b, 0, 0))),
        out_shape=(jax.ShapeDtypeStruct((m, POOL_WIDTH + CONV_WIDTH), BF16),
                   jax.ShapeDtypeStruct((nseq, POOL_STATE, POOL_WIDTH), F32),
                   jax.ShapeDtypeStruct((nseq, CONV_K - 1, CONV_WIDTH), F32)),
        scratch_shapes=[pltpu.VMEM((POOL_HALO + ts, POOL_WIDTH), F32),
                        pltpu.VMEM((CONV_HALO + ts, CONV_WIDTH), F32)],
        compiler_params=_cparams(("arbitrary", "arbitrary")),
        name="prompt_mixers",
    )(proj, proj, proj, proj, pool_w_b, pool_scale, conv_w, g_pool, g_conv)


def _fox_kernel(q_ref, k_ref, v_ref, cq_ref, ck_ref, g_ref, *rest, tq, tk, n_casts):
    cast_src = rest[:n_casts]
    out_ref = rest[n_casts]
    cast_dst = rest[n_casts + 1:2 * n_casts + 1]
    m_scr, l_scr, cq_scr, acc_scr = rest[2 * n_casts + 1:]
    _run_casts(cast_src, cast_dst)
    i = pl.program_id(1)
    causal = (lax.broadcasted_iota(jnp.int32, (tq, tq), 1) <= lax.broadcasted_iota(jnp.int32, (tq, tq), 0))

    for h in range(N_HEADS):
        m_scr[h] = jnp.full((tq, LANES), -jnp.inf, F32)
        l_scr[h] = jnp.zeros((tq, LANES), F32)
        cq_scr[h] = jnp.broadcast_to(cq_ref[0, :, h:h + 1], (tq, LANES))
    acc_scr[...] = jnp.zeros(acc_scr.shape, F32)

    def step(k0, width, masked):
        ks = pl.ds(pl.multiple_of(k0, width), width)
        lanes = lambda x: jnp.concatenate([x] * (width // LANES), axis=1)
        ones = jnp.ones((width, HEAD_DIM), BF16)
        for h in range(N_HEADS):
            cs = slice(h * HEAD_DIM, (h + 1) * HEAD_DIM)
            s = lax.dot_general(q_ref[:, cs], k_ref[ks, cs], (((1,), (1,)), ((), ())),
                                preferred_element_type=F32)
            s = s + lanes(cq_scr[h]) - ck_ref[0, h:h + 1, ks]
            if masked:
                s = jnp.where(causal, s, -jnp.inf)
            m_prev = m_scr[h]
            m_new = jnp.maximum(m_prev, jnp.broadcast_to(jnp.max(s, axis=1, keepdims=True), (tq, LANES)))
            alpha = jnp.exp(m_prev - m_new)
            p = jnp.exp(s - lanes(m_new)).astype(BF16)
            pv = jnp.dot(p, jnp.concatenate([v_ref[ks, cs], ones], axis=1), preferred_element_type=F32)
            l_scr[h] = alpha * l_scr[h] + pv[:, HEAD_DIM:]
            acc_scr[:, cs] = alpha * acc_scr[:, cs] + pv[:, :HEAD_DIM]
            m_scr[h] = m_new

    def body(j, c):
        step(j * tk, tk, False)
        return c

    n_wide = (i * tq) // tk
    lax.fori_loop(0, n_wide, body, 0)

    @pl.when(n_wide * tk < i * tq)
    def _():
        step(n_wide * tk, tq, False)

    step(i * tq, tq, True)

    for h in range(N_HEADS):
        cs = slice(h * HEAD_DIM, (h + 1) * HEAD_DIM)
        acc_scr[:, cs] = acc_scr[:, cs] / l_scr[h]
    out_ref[...] = _rms(acc_scr[...], g_ref[...]).astype(BF16)


def _fox_prompt(qb, kb, vb, ccol, crow, g_att, nseq, s, cast_jobs=(), tq=256, tk=512):
    assert tk == 2 * tq and s % tk == 0
    m = qb.shape[0]
    nq = s // tq
    stat = pltpu.VMEM((N_HEADS, tq, LANES), F32)
    cast_in, cast_out, cast_shapes = _cast_plumbing(cast_jobs, lambda b, i: b * nq + i)
    assert all(j.src.shape[1] // j.chunk_rows == nseq * nq for j in cast_jobs)
    return pl.pallas_call(
        functools.partial(_fox_kernel, tq=tq, tk=tk, n_casts=len(cast_jobs)),
        grid=(nseq, nq),
        in_specs=[pl.BlockSpec((tq, ATT_WIDTH), lambda b, i: (b * nq + i, 0)),
                  pl.BlockSpec((s, ATT_WIDTH), lambda b, i: (b, 0)),
                  pl.BlockSpec((s, ATT_WIDTH), lambda b, i: (b, 0)),
                  pl.BlockSpec((1, tq, N_HEADS), lambda b, i: (b, i, 0)),
                  pl.BlockSpec((1, N_HEADS, s), lambda b, i: (b, 0, 0)),
                  pl.BlockSpec((1, ATT_WIDTH), lambda b, i: (0, 0))] + cast_in,
        out_specs=[pl.BlockSpec((tq, ATT_WIDTH), lambda b, i: (b * nq + i, 0))] + cast_out,
        out_shape=[jax.ShapeDtypeStruct((m, ATT_WIDTH), BF16)] + cast_shapes,
        scratch_shapes=[stat, stat, stat, pltpu.VMEM((tq, ATT_WIDTH), F32)],
        compiler_params=_cparams(("arbitrary", "arbitrary")),
        name="fox_prompt",
    )(qb, kb, vb, ccol, crow, g_att, *[j.src for j in cast_jobs])


def _outproj_kernel(x_ref, ma_ref, mb_ref, wa_ref, wb_ref, o_ref):
    acc = jnp.dot(ma_ref[...], wa_ref[...], preferred_element_type=F32)
    acc = acc + jnp.dot(mb_ref[...], wb_ref[...], preferred_element_type=F32)
    o_ref[...] = x_ref[...] + acc


def _outproj(x, mix_a, mix_b, w_o_b, tm, tn=512):
    m = x.shape[0]
    half = D_MODEL // 2
    return pl.pallas_call(
        _outproj_kernel,
        grid=(m // tm, D_MODEL // tn),
        in_specs=[pl.BlockSpec((tm, tn), lambda i, n: (i, n)),
                  pl.BlockSpec((tm, half), lambda i, n: (i, 0)),
                  pl.BlockSpec((tm, half), lambda i, n: (i, 0)),
                  pl.BlockSpec((half, tn), lambda i, n: (0, n)),
                  pl.BlockSpec((half, tn), lambda i, n: (1, n))],
        out_specs=pl.BlockSpec((tm, tn), lambda i, n: (i, n)),
        out_shape=jax.ShapeDtypeStruct((m, D_MODEL), F32),
        compiler_params=_cparams(("arbitrary", "arbitrary")),
        name="outproj",
    )(x, mix_a, mix_b, w_o_b, w_o_b)


FFN_HALO = 8


def _ffn_kernel(*refs, tm, nf, tiles_per_seq, per_row_state):
    if per_row_state:
        (x_ref, g2_ref, wa_ref, wg_ref, cw_ref, cb_ref, wd_ref, prev_ref,
         o_ref, state_ref, xn_ref, act_scr, abuf, carry_ref) = refs
    else:
        (x_ref, g2_ref, wa_ref, wg_ref, cw_ref, cb_ref, wd_ref,
         o_ref, state_ref, xn_ref, act_scr, abuf, carry_ref) = refs
    mi = pl.program_id(0)
    f = pl.program_id(1)
    rc = min(tm, 256)

    @pl.when(f == 0)
    def _():
        def body(i, c):
            r = pl.ds(pl.multiple_of(i * rc, rc), rc)
            x = x_ref[r, :]
            xn_ref[r, :] = _rms(x, g2_ref[...]).astype(BF16)
            o_ref[r, :] = x
            return c
        lax.fori_loop(0, tm // rc, body, 0)

        @pl.when(mi == 0)
        def _():
            carry_ref[...] = jnp.zeros(carry_ref.shape, F32)

    def up():
        xn = xn_ref[...]
        a = jnp.dot(xn, wa_ref[...], preferred_element_type=F32)
        gate = jnp.dot(xn, wg_ref[...], preferred_element_type=F32)
        if per_row_state:
            p1 = prev_ref[1]
            ac = cw_ref[0:1, :] * prev_ref[0] + cw_ref[1:2, :] * p1 + cw_ref[2:3, :] * a
            state_ref[0] = p1
            state_ref[1] = a
        else:
            first = (mi % tiles_per_seq) == 0
            abuf[0:FFN_HALO, :] = jnp.where(first, 0.0, carry_ref[f])
            abuf[FFN_HALO:FFN_HALO + tm, :] = a
            carry_ref[f] = a[tm - FFN_HALO:tm, :]
            ac = (cw_ref[0:1, :] * abuf[FFN_HALO - 2:FFN_HALO - 2 + tm, :]
                  + cw_ref[1:2, :] * abuf[FFN_HALO - 1:FFN_HALO - 1 + tm, :]
                  + cw_ref[2:3, :] * a)
            state_ref[0] = a[tm - (FFN_K - 1):tm, :]
        ac = ac + cb_ref[...]
        act_scr[f % 2] = (ac * jax.nn.sigmoid(ac) * gate).astype(BF16)

    def down():
        o_ref[...] += jnp.dot(act_scr[(f + 1) % 2], wd_ref[...], preferred_element_type=F32)

    @pl.when(f == 0)
    def _():
        up()

    @pl.when((f > 0) & (f < nf))
    def _():
        down()
        up()

    @pl.when(f == nf)
    def _():
        down()


def _ffn(x, g2, w_up_b, cw, cb, w_down_b, prev, tm, seq_len, tf=512):
    m = x.shape[0]
    nf = D_FF // tf
    per_row_state = prev is not None
    cur = lambda f: jnp.minimum(f, nf - 1)
    old = lambda f: jnp.maximum(f - 1, 0)
    in_specs = [pl.BlockSpec((tm, D_MODEL), lambda i, f: (i, 0), pipeline_mode=pl.Buffered(1)),
                pl.BlockSpec((1, D_MODEL), lambda i, f: (0, 0)),
                pl.BlockSpec((D_MODEL, tf), lambda i, f: (0, cur(f))),
                pl.BlockSpec((D_MODEL, tf), lambda i, f: (0, nf + cur(f))),
                pl.BlockSpec((FFN_K, tf), lambda i, f: (0, cur(f))),
                pl.BlockSpec((1, tf), lambda i, f: (0, cur(f))),
                pl.BlockSpec((tf, D_MODEL), lambda i, f: (old(f), 0))]
    args = [x, g2, w_up_b, w_up_b, cw, cb, w_down_b]
    if per_row_state:
        tiles_per_seq = 1
        in_specs.append(pl.BlockSpec((FFN_K - 1, tm, tf), lambda i, f: (0, i, cur(f))))
        args.append(prev)
        state_spec = pl.BlockSpec((FFN_K - 1, tm, tf), lambda i, f: (0, i, cur(f)))
        state_shape = jax.ShapeDtypeStruct((FFN_K - 1, m, D_FF), F32)
    else:
        tiles_per_seq = seq_len // tm
        state_spec = pl.BlockSpec((1, FFN_K - 1, tf), lambda i, f: (i, 0, cur(f)))
        state_shape = jax.ShapeDtypeStruct((m // tm, FFN_K - 1, D_FF), F32)
    x_out, state = pl.pallas_call(
        functools.partial(_ffn_kernel, tm=tm, nf=nf, tiles_per_seq=tiles_per_seq, per_row_state=per_row_state),
        grid=(m // tm, nf + 1),
        in_specs=in_specs,
        out_specs=(pl.BlockSpec((tm, D_MODEL), lambda i, f: (i, 0)), state_spec),
        out_shape=(jax.ShapeDtypeStruct((m, D_MODEL), F32), state_shape),
        scratch_shapes=[pltpu.VMEM((tm, D_MODEL), BF16),
                        pltpu.VMEM((2, tm, tf), BF16),
                        pltpu.VMEM((FFN_HALO + tm, tf), F32),
                        pltpu.VMEM((nf, FFN_HALO, tf), F32)],
        compiler_params=_cparams(("arbitrary", "arbitrary")),
        name="ffn",
    )(*args)
    if not per_row_state:
        state = state[tiles_per_seq - 1::tiles_per_seq]
    return x_out, state


PAGE_ROW = PAGE_SIZE * N_HEADS


def _bias_kernel(pt_ref, lf_hbm, lfnew_ref, bias_ref, lfbuf, sem, *, db, n_pages):
    n_rows = db * n_pages

    def copy(r):
        return pltpu.make_async_copy(lf_hbm.at[pl.ds(pt_ref[r], 1)], lfbuf.at[pl.ds(r, 1)], sem)

    def start(r, c):
        copy(r).start()
        return c

    def wait(r, c):
        copy(r).wait()
        return c

    lax.fori_loop(0, n_rows, start, 0)
    lax.fori_loop(0, n_rows, wait, 0)

    src = lax.broadcasted_iota(jnp.int32, (PAGE_ROW, PAGE_ROW), 0)
    dst = lax.broadcasted_iota(jnp.int32, (PAGE_ROW, PAGE_ROW), 1)
    same_head = (src & (N_HEADS - 1)) == (dst & (N_HEADS - 1))
    later = (src >> 3) > (dst >> 3)
    m_suffix = (same_head & later).astype(BF16)
    m_total = same_head.astype(BF16)
    pr = lax.broadcasted_iota(jnp.int32, (n_pages, n_pages), 0)
    pc = lax.broadcasted_iota(jnp.int32, (n_pages, n_pages), 1)
    later_page = (pc > pr).astype(BF16)

    pieces = _split3(lfbuf[...])
    within = _dot01(pieces, m_suffix)
    total = _dot01(pieces, m_total)
    for b in range(db):
        rows = slice(b * n_pages, (b + 1) * n_pages)
        t0, t1, t2 = _split3(total[rows])
        after = jnp.dot(later_page, t0, preferred_element_type=F32)
        after = after + jnp.dot(later_page, t1, preferred_element_type=F32)
        after = after + jnp.dot(later_page, t2, preferred_element_type=F32)
        bias_ref[b] = within[rows] + after + lfnew_ref[b]


def _sample_bias(page_table, logf_flat, lfnew_rep):
    db, n_pages = page_table.shape
    grid_spec = pltpu.PrefetchScalarGridSpec(
        num_scalar_prefetch=1,
        grid=(1,),
        in_specs=[pl.BlockSpec(memory_space=pl.ANY),
                  pl.BlockSpec((db, 1, PAGE_ROW), lambda i, pt: (0, 0, 0))],
        out_specs=pl.BlockSpec((db, n_pages, PAGE_ROW), lambda i, pt: (0, 0, 0)),
        scratch_shapes=[pltpu.VMEM((db * n_pages, PAGE_ROW), F32), pltpu.SemaphoreType.DMA(())],
    )
    return pl.pallas_call(
        functools.partial(_bias_kernel, db=db, n_pages=n_pages),
        grid_spec=grid_spec,
        out_shape=jax.ShapeDtypeStruct((db, n_pages, PAGE_ROW), F32),
        compiler_params=_cparams(("arbitrary",)),
        name="sample_bias",
    )(page_table.reshape(db * n_pages), logf_flat, lfnew_rep)


def _head_allreduce(x, op, reduce_op):
    x = jnp.broadcast_to(reduce_op(x, axis=0, keepdims=True), (SUBLANES, LANES))
    for sh in (8, 16, 32, 64):
        x = op(x, pltpu.roll(x, sh, axis=1))
    return x


def _paged_kernel(pt_ref, q_ref, bias_ref, knew_ref, vnew_ref, *refs, n_steps):
    k_refs = refs[:PAGES_PER_STEP]
    v_refs = refs[PAGES_PER_STEP:2 * PAGES_PER_STEP]
    o_ref = refs[2 * PAGES_PER_STEP]
    m_scr, l_scr, acc_scr, s_scr = refs[2 * PAGES_PER_STEP + 1:]
    i = pl.program_id(1)

    @pl.when(i == 0)
    def _():
        m_scr[...] = jnp.full((SUBLANES, LANES), -jnp.inf, F32)
        l_scr[...] = jnp.zeros((SUBLANES, LANES), F32)
        acc_scr[...] = jnp.zeros((N_HEADS, HEAD_DIM), F32)

    q = q_ref[0]
    sub = lax.broadcasted_iota(jnp.int32, (N_HEADS, PAGE_ROW), 0)
    lane = lax.broadcasted_iota(jnp.int32, (N_HEADS, PAGE_ROW), 1)
    own_head = sub == (lane & (N_HEADS - 1))

    for j in range(PAGES_PER_STEP):
        k2 = k_refs[j][...].reshape(PAGE_ROW, HEAD_DIM).astype(BF16)
        st = lax.dot_general(q, k2, (((1,), (1,)), ((), ())), preferred_element_type=F32)
        s_scr[j:j + 1, :] = jnp.sum(jnp.where(own_head, st, 0.0), axis=0, keepdims=True)

    s = s_scr[...] + bias_ref[0]
    chunks = [s[:, c * LANES:(c + 1) * LANES] for c in range(PAGE_ROW // LANES)]
    mx = chunks[0]
    for c in chunks[1:]:
        mx = jnp.maximum(mx, c)
    m_prev = m_scr[...]
    m_new = jnp.maximum(m_prev, _head_allreduce(mx, jnp.maximum, jnp.max))
    alpha = jnp.exp(m_prev - m_new)
    p = jnp.exp(s - jnp.concatenate([m_new[0:1, :]] * (PAGE_ROW // LANES), axis=1))
    ps = p[:, 0:LANES]
    for c in range(1, PAGE_ROW // LANES):
        ps = ps + p[:, c * LANES:(c + 1) * LANES]
    l_scr[...] = alpha * l_scr[...] + _head_allreduce(ps, jnp.add, jnp.sum)
    m_scr[...] = m_new

    o = jnp.zeros((N_HEADS, HEAD_DIM), F32)
    for j in range(PAGES_PER_STEP):
        pj = jnp.where(own_head, jnp.broadcast_to(p[j:j + 1, :], (N_HEADS, PAGE_ROW)), 0.0).astype(BF16)
        v2 = v_refs[j][...].reshape(PAGE_ROW, HEAD_DIM).astype(BF16)
        o = o + jnp.dot(pj, v2, preferred_element_type=F32)

    sub8 = lax.broadcasted_iota(jnp.int32, (SUBLANES, LANES), 0)
    lane8 = lax.broadcasted_iota(jnp.int32, (SUBLANES, LANES), 1)
    diag = sub8 == lane8

    def to_col(x):
        return jnp.sum(jnp.where(diag, x, 0.0), axis=1, keepdims=True)

    acc = to_col(alpha) * acc_scr[...] + o
    acc_scr[...] = acc

    @pl.when(i == n_steps - 1)
    def _():
        m_col = to_col(m_new)
        l_col = to_col(l_scr[...])
        qf = q.astype(F32)
        kn = knew_ref[0].astype(BF16).astype(F32)
        s_new = jnp.sum(qf * kn, axis=1, keepdims=True)
        m_f = jnp.maximum(m_col, s_new)
        a_old = jnp.exp(m_col - m_f)
        p_new = jnp.exp(s_new - m_f)
        o_ref[0] = (acc * a_old + p_new * vnew_ref[0]) / (l_col * a_old + p_new)


def _paged_attention(layer, page_table, q3, bias, k_new3, v_new3, cache_k, cache_v):
    db, n_pages = page_table.shape
    n_steps = n_pages // PAGES_PER_STEP

    def page_spec(j):
        return pl.BlockSpec((None, None, PAGE_SIZE, N_HEADS, HEAD_DIM),
                            lambda b, i, pt: (layer, pt[b, i * PAGES_PER_STEP + j], 0, 0, 0))

    row = pl.BlockSpec((1, N_HEADS, HEAD_DIM), lambda b, i, pt: (b, 0, 0))
    grid_spec = pltpu.PrefetchScalarGridSpec(
        num_scalar_prefetch=1,
        grid=(db, n_steps),
        in_specs=[row,
                  pl.BlockSpec((1, PAGES_PER_STEP, PAGE_ROW), lambda b, i, pt: (b, i, 0)),
                  row, row]
                 + [page_spec(j) for j in range(PAGES_PER_STEP)]
                 + [page_spec(j) for j in range(PAGES_PER_STEP)],
        out_specs=row,
        scratch_shapes=[pltpu.VMEM((SUBLANES, LANES), F32), pltpu.VMEM((SUBLANES, LANES), F32),
                        pltpu.VMEM((N_HEADS, HEAD_DIM), F32), pltpu.VMEM((PAGES_PER_STEP, PAGE_ROW), F32)],
    )
    return pl.pallas_call(
        functools.partial(_paged_kernel, n_steps=n_steps),
        grid_spec=grid_spec,
        out_shape=jax.ShapeDtypeStruct((db, N_HEADS, HEAD_DIM), F32),
        compiler_params=_cparams(("arbitrary", "arbitrary")),
        name="paged_attention",
    )(page_table, q3, bias, k_new3, v_new3, *([cache_k] * PAGES_PER_STEP), *([cache_v] * PAGES_PER_STEP))


def _sample_mix_kernel(proj_ref, sp_ref, sc_ref, yatt_ref, pw_ref, ps_ref, cw_ref, gp_ref, gc_ref, ga_ref,
                       mixa_ref, mixb_ref, pstate_ref, cstate_ref):
    u = proj_ref[:, 0:POOL_WIDTH]
    hb = proj_ref[:, POOL_WIDTH:POOL_WIDTH + CONV_WIDTH]
    hc = proj_ref[:, POOL_WIDTH + CONV_WIDTH:POOL_WIDTH + 2 * CONV_WIDTH]
    hx = proj_ref[:, POOL_WIDTH + 2 * CONV_WIDTH:POOL_WIDTH + 3 * CONV_WIDTH]

    ys = []
    for g, win in enumerate(POOL_WINDOWS):
        cols = slice(g * POOL_GROUP_DIM, (g + 1) * POOL_GROUP_DIM)
        ug = u[:, cols]
        acc = ug
        for j in range(1, win):
            acc = acc + sp_ref[POOL_STATE - j, :, cols]
        cnt = float(min(POOL_STATE + 1, win))
        pooled = (acc / cnt - ug).astype(BF16)
        ys.append(jnp.dot(pooled, pw_ref[g], preferred_element_type=F32))
    y_pool = jnp.concatenate(ys, axis=1) * ps_ref[...]
    mixa_ref[:, 0:POOL_WIDTH] = _rms(y_pool, gp_ref[...]).astype(BF16)

    z = hc * hx
    conv = cw_ref[0:1, :] * sc_ref[0] + cw_ref[1:2, :] * sc_ref[1] + cw_ref[2:3, :] * z
    mixa_ref[:, POOL_WIDTH:POOL_WIDTH + CONV_WIDTH] = _rms(hb * conv, gc_ref[...]).astype(BF16)
    mixb_ref[...] = _rms(yatt_ref[...], ga_ref[...]).astype(BF16)

    for r in range(POOL_STATE - 1):
        pstate_ref[r] = sp_ref[r + 1]
    pstate_ref[POOL_STATE - 1] = u
    cstate_ref[0] = sc_ref[1]
    cstate_ref[1] = z


def _sample_mixers(proj, sp_t, sc_t, y_att, pool_w_b, pool_scale, conv_w, g_pool, g_conv, g_att):
    db = proj.shape[0]
    return pl.pallas_call(
        _sample_mix_kernel,
        out_shape=(jax.ShapeDtypeStruct((db, POOL_WIDTH + CONV_WIDTH), BF16),
                   jax.ShapeDtypeStruct((db, ATT_WIDTH), BF16),
                   jax.ShapeDtypeStruct((POOL_STATE, db, POOL_WIDTH), F32),
                   jax.ShapeDtypeStruct((CONV_K - 1, db, CONV_WIDTH), F32)),
        compiler_params=pltpu.CompilerParams(vmem_limit_bytes=VMEM_LIMIT),
        name="sample_mixers",
    )(proj, sp_t, sc_t, y_att, pool_w_b, pool_scale, conv_w, g_pool, g_conv, g_att)


def kernel(x_prompt, x_sample, cache_k, cache_v, cache_logf, state_pool, state_conv, state_ffn, page_table,
           norm1_g, w_in, b_f, pool_w, pool_scale, conv_w, q_norm_g, k_norm_g, out_norm_g, w_o, norm2_g,
           w_up, ffn_conv_w, ffn_conv_b, w_down):
    nb, s, _ = x_prompt.shape
    db = x_sample.shape[0]
    depth = w_in.shape[0]
    n_pool = cache_k.shape[1]

    xp = x_prompt.reshape(nb * s, D_MODEL)
    xs = x_sample.reshape(db, D_MODEL)
    logf_flat = cache_logf.reshape(depth, n_pool, PAGE_ROW)

    w_in_cols = D_PROJ + LANES
    w_in_b = jnp.pad(w_in[0], ((0, 0), (0, w_in_cols - w_in.shape[2]))).astype(BF16)
    n_fox_steps = nb * (s // FOX_TQ)

    outs = {name: [] for name in ("fp", "pp", "cp", "ffp", "fs", "ps", "cs", "ffs")}
    kv_prompt = (jnp.zeros((depth, nb * s, ATT_WIDTH), F32), jnp.ones((depth, nb * s, ATT_WIDTH), F32))
    kv_sample = (jnp.zeros((depth, db, ATT_WIDTH), F32), jnp.ones((depth, db, ATT_WIDTH), F32))
    for l in range(depth):
        bf_pad = jnp.pad(b_f[l], (0, LANES - N_HEADS)).reshape(1, LANES)
        pool_w_b = pool_w[l].astype(BF16)
        g1 = norm1_g[l].reshape(1, D_MODEL)
        g2 = norm2_g[l].reshape(1, D_MODEL)
        qg = q_norm_g[l].reshape(1, HEAD_DIM)
        kg = k_norm_g[l].reshape(1, HEAD_DIM)
        ps_l = pool_scale[l].reshape(1, POOL_WIDTH)
        g_pool = out_norm_g[l, :POOL_WIDTH].reshape(1, POOL_WIDTH)
        g_conv = out_norm_g[l, POOL_WIDTH:POOL_WIDTH + CONV_WIDTH].reshape(1, CONV_WIDTH)
        g_att = out_norm_g[l, POOL_WIDTH + CONV_WIDTH:].reshape(1, ATT_WIDTH)
        cb = ffn_conv_b[l].reshape(1, D_FF)

        next_w_in = ([_CastJob(w_in, l + 1, D_MODEL // IN_CAST_STEPS, w_in_cols)] if l + 1 < depth else [])
        proj, qb, kp_stack, kb, vp_stack, vb, logf, logf_pad, *w_in_next = _inproj(
            l, xp, g1, w_in_b, bf_pad, qg, kg, tm=1024, kv_stack=kv_prompt, cast_jobs=next_w_in)
        kv_prompt = (kp_stack, vp_stack)
        ccol, crow = _cumsum(logf_pad, nb, s)
        mix_a, pstate, cstate = _mixers(proj, pool_w_b, ps_l, conv_w[l], g_pool, g_conv, nb, s)
        own_weights = [_CastJob(w_o, l, D_MODEL // n_fox_steps, D_MODEL),
                       _CastJob(w_up, l, D_MODEL // n_fox_steps, 2 * D_FF),
                       _CastJob(w_down, l, D_FF // n_fox_steps, D_MODEL)]
        mix_b, w_o_b, w_up_b, w_down_b = _fox_prompt(qb, kb, vb, ccol, crow, g_att, nb, s, cast_jobs=own_weights,
                                                     tq=FOX_TQ, tk=2 * FOX_TQ)
        x_mid = _outproj(xp, mix_a, mix_b, w_o_b, tm=1024)
        xp, fstate = _ffn(x_mid, g2, w_up_b, ffn_conv_w[l], cb, w_down_b, None, tm=PROMPT_FFN_TM, seq_len=s)
        outs["fp"].append(logf.reshape(nb, s, N_HEADS))
        outs["pp"].append(pstate)
        outs["cp"].append(cstate)
        outs["ffp"].append(fstate)

        proj_s, qb_s, ks_stack, _, vs_stack, _, logf_s, _ = _inproj(
            l, xs, g1, w_in_b, bf_pad, qg, kg, tm=db, kv_stack=kv_sample)
        kv_sample = (ks_stack, vs_stack)
        k32_s, v32_s = ks_stack[l], vs_stack[l]
        lfnew_rep = jnp.tile(logf_s, (1, PAGE_SIZE)).reshape(db, 1, PAGE_ROW)
        bias = _sample_bias(page_table, logf_flat[l], lfnew_rep)
        y_att = _paged_attention(l, page_table, qb_s.reshape(db, N_HEADS, HEAD_DIM), bias,
                                 k32_s.reshape(db, N_HEADS, HEAD_DIM), v32_s.reshape(db, N_HEADS, HEAD_DIM),
                                 cache_k, cache_v)
        mix_a_s, mix_b_s, pstate_t, cstate_t = _sample_mixers(
            proj_s, jnp.swapaxes(state_pool[l], 0, 1), jnp.swapaxes(state_conv[l], 0, 1),
            y_att.reshape(db, ATT_WIDTH), pool_w_b, ps_l, conv_w[l], g_pool, g_conv, g_att)
        xs_mid = _outproj(xs, mix_a_s, mix_b_s, w_o_b, tm=db)
        xs, fstate_t = _ffn(xs_mid, g2, w_up_b, ffn_conv_w[l], cb, w_down_b,
                            jnp.swapaxes(state_ffn[l], 0, 1), tm=db, seq_len=1)
        if w_in_next:
            w_in_b = w_in_next[0]
        outs["fs"].append(logf_s.reshape(db, 1, N_HEADS))
        outs["ps"].append(jnp.swapaxes(pstate_t, 0, 1))
        outs["cs"].append(jnp.swapaxes(cstate_t, 0, 1))
        outs["ffs"].append(jnp.swapaxes(fstate_t, 0, 1))

    st = lambda name: jnp.stack(outs[name])
    return (xp.reshape(nb, s, D_MODEL), xs.reshape(db, 1, D_MODEL),
            kv_prompt[0].reshape(depth, nb, s, N_HEADS, HEAD_DIM), kv_prompt[1].reshape(depth, nb, s, N_HEADS, HEAD_DIM),
            st("fp"), st("pp"), st("cp"), st("ffp"),
            kv_sample[0].reshape(depth, db, 1, N_HEADS, HEAD_DIM), kv_sample[1].reshape(depth, db, 1, N_HEADS, HEAD_DIM),
            st("fs"), st("ps"), st("cs"), st("ffs"))
```

```python
import functools
from typing import NamedTuple

import jax
import jax.numpy as jnp
from jax import lax
from jax.experimental import pallas as pl
from jax.experimental.pallas import tpu as pltpu

F32 = jnp.float32
BF16 = jnp.bfloat16

D_MODEL = 2048
PAGE_SIZE = 128
POOL_WIDTH = 512
POOL_WINDOWS = (2, 4, 8, 16)
POOL_GROUP_DIM = 128
POOL_STATE = 15
CONV_WIDTH = 512
CONV_K = 3
ATT_WIDTH = 1024
HEAD_DIM = 128
N_HEADS = 8
D_FF = 5632
FFN_K = 3
EPS = 1e-6
D_PROJ = POOL_WIDTH + 3 * CONV_WIDTH + 3 * ATT_WIDTH
SCALE = HEAD_DIM ** -0.5

LANES = 128
SUBLANES = 8
VMEM_LIMIT = 56 * 1024 * 1024

PROJ_TN = 512
PROMPT_FFN_TM = 1024
FOX_TQ = 256
LOG2E = 1.4426950408889634

T_Q = (POOL_WIDTH + 3 * CONV_WIDTH) // PROJ_TN
T_K = T_Q + ATT_WIDTH // PROJ_TN
T_V = T_K + ATT_WIDTH // PROJ_TN
T_END = D_PROJ // PROJ_TN
PAGES_PER_STEP = 16


def _cparams(sem):
    return pltpu.CompilerParams(dimension_semantics=sem, vmem_limit_bytes=VMEM_LIMIT)


def _rms(x, g):
    ms = jnp.mean(x * x, axis=-1, keepdims=True)
    return x * lax.rsqrt(ms + EPS) * g


def _log_sigmoid(z):
    return jnp.minimum(z, 0.0) - jnp.log1p(jnp.exp(-jnp.abs(z)))


def _split3(x):
    p0 = x.astype(BF16)
    r = x - p0.astype(F32)
    p1 = r.astype(BF16)
    p2 = (r - p1.astype(F32)).astype(BF16)
    return p0, p1, p2


def _dot01(a_pieces, b):
    out = jnp.dot(a_pieces[0], b, preferred_element_type=F32)
    for p in a_pieces[1:]:
        out = out + jnp.dot(p, b, preferred_element_type=F32)
    return out


class _CastJob(NamedTuple):
    src: jax.Array
    layer: int
    chunk_rows: int
    out_cols: int


def _cast_plumbing(jobs, step_of):
    in_specs, out_specs, out_shapes = [], [], []
    for job in jobs:
        _, rows, cols = job.src.shape
        n_chunks = rows // job.chunk_rows
        chunk = lambda *g, n_chunks=n_chunks: jnp.minimum(step_of(*g), n_chunks - 1)
        in_specs.append(pl.BlockSpec((None, job.chunk_rows, cols),
                                     lambda *g, job=job, chunk=chunk: (job.layer, chunk(*g), 0)))
        out_specs.append(pl.BlockSpec((job.chunk_rows, job.out_cols), lambda *g, chunk=chunk: (chunk(*g), 0)))
        out_shapes.append(jax.ShapeDtypeStruct((rows, job.out_cols), BF16))
    return in_specs, out_specs, out_shapes


def _run_casts(src_refs, dst_refs):
    for src, dst in zip(src_refs, dst_refs):
        cols = src.shape[1]
        if dst.shape[1] > cols:
            dst[...] = jnp.zeros(dst.shape, BF16)
        dst[:, :cols] = src[...].astype(BF16)


def _inproj_kernel(x_ref, g1_ref, w_ref, wf_ref, bf_ref, qg_ref, kg_ref, *rest, tm, n_aliased, n_casts, cast_steps):
    cast_src = rest[n_aliased:n_aliased + n_casts]
    proj_ref, q_ref, k32_ref, kb_ref, v32_ref, vb_ref, logf_ref, logfp_ref = rest[n_aliased + n_casts:][:8]
    cast_dst = rest[n_aliased + n_casts + 8:][:n_casts]
    xn_ref, y_scr = rest[-2:]
    n = pl.program_id(1)
    rc = min(tm, 256)

    if n_casts:
        @pl.when(pl.program_id(0) * pl.num_programs(1) + n < cast_steps)
        def _():
            _run_casts(cast_src, cast_dst)

    def normalise():
        def body(i, c):
            r = pl.ds(pl.multiple_of(i * rc, rc), rc)
            xn_ref[r, :] = _rms(x_ref[r, :], g1_ref[...]).astype(BF16)
            return c
        lax.fori_loop(0, tm // rc, body, 0)

    def matmul():
        return jnp.dot(xn_ref[...], w_ref[...], preferred_element_type=F32)

    def park():
        y_scr[n % 2] = matmul()

    def head_norm(g):
        y = y_scr[(n + 1) % 2]
        parts = []
        for h in range(PROJ_TN // HEAD_DIM):
            parts.append(_rms(y[:, h * HEAD_DIM:(h + 1) * HEAD_DIM], g))
        return jnp.concatenate(parts, axis=1)

    def finish_q():
        q_ref[...] = (head_norm(qg_ref[...]) * (SCALE * LOG2E)).astype(BF16)

    def finish_k():
        kn = head_norm(kg_ref[...])
        k32_ref[...] = kn
        kb_ref[...] = kn.astype(BF16)

    def direct_v():
        y = matmul()
        v32_ref[...] = y
        vb_ref[...] = y.astype(BF16)

    @pl.when(n == 0)
    def _():
        normalise()
        fl = jnp.dot(xn_ref[...], wf_ref[...], preferred_element_type=F32)
        lf = _log_sigmoid(fl + bf_ref[...])
        lane = lax.broadcasted_iota(jnp.int32, lf.shape, 1)
        lf = jnp.where(lane < N_HEADS, lf, 0.0)
        logfp_ref[...] = lf
        logf_ref[...] = lf[:, :N_HEADS]

    @pl.when(n < T_Q)
    def _():
        proj_ref[...] = matmul()

    @pl.when(n == T_Q)
    def _():
        park()

    @pl.when((n > T_Q) & (n <= T_K))
    def _():
        finish_q()
        park()

    @pl.when((n > T_K) & (n < T_V))
    def _():
        finish_k()
        park()

    @pl.when(n == T_V)
    def _():
        finish_k()
        direct_v()

    @pl.when(n > T_V)
    def _():
        direct_v()


def _inproj(layer, x, g1, w_in_b, bf_pad, qg, kg, tm, kv_stack, cast_jobs=()):
    m = x.shape[0]
    depth = kv_stack[0].shape[0]
    n_steps = T_END
    col = lambda first_step: (lambda i, n: (i, jnp.clip(n - first_step, 0, 1)))
    slab = lambda first_step: (lambda i, n: (layer, i, jnp.clip(n - first_step, 0, 1)))
    out_shape = (
        jax.ShapeDtypeStruct((m, 4 * PROJ_TN), F32),
        jax.ShapeDtypeStruct((m, ATT_WIDTH), BF16),
        jax.ShapeDtypeStruct((depth, m, ATT_WIDTH), F32),
        jax.ShapeDtypeStruct((m, ATT_WIDTH), BF16),
        jax.ShapeDtypeStruct((depth, m, ATT_WIDTH), F32),
        jax.ShapeDtypeStruct((m, ATT_WIDTH), BF16),
        jax.ShapeDtypeStruct((m, N_HEADS), F32),
        jax.ShapeDtypeStruct((m, LANES), F32),
    )
    out_specs = (
        pl.BlockSpec((tm, PROJ_TN), lambda i, n: (i, jnp.minimum(n, T_Q - 1))),
        pl.BlockSpec((tm, PROJ_TN), col(T_Q + 1)),
        pl.BlockSpec((None, tm, PROJ_TN), slab(T_K + 1)),
        pl.BlockSpec((tm, PROJ_TN), col(T_K + 1)),
        pl.BlockSpec((None, tm, PROJ_TN), slab(T_V)),
        pl.BlockSpec((tm, PROJ_TN), col(T_V)),
        pl.BlockSpec((tm, N_HEADS), lambda i, n: (i, 0)),
        pl.BlockSpec((tm, LANES), lambda i, n: (i, 0)),
    )
    in_specs = [
        pl.BlockSpec((tm, D_MODEL), lambda i, n: (i, 0)),
        pl.BlockSpec((1, D_MODEL), lambda i, n: (0, 0)),
        pl.BlockSpec((D_MODEL, PROJ_TN), lambda i, n: (0, n)),
        pl.BlockSpec((D_MODEL, LANES), lambda i, n: (0, D_PROJ // LANES)),
        pl.BlockSpec((1, LANES), lambda i, n: (0, 0)),
        pl.BlockSpec((1, HEAD_DIM), lambda i, n: (0, 0)),
        pl.BlockSpec((1, HEAD_DIM), lambda i, n: (0, 0)),
        pl.BlockSpec(memory_space=pl.ANY), pl.BlockSpec(memory_space=pl.ANY),
    ]
    args = [x, g1, w_in_b, w_in_b, bf_pad, qg, kg, *kv_stack]
    aliases = {7: 2, 8: 4}
    cast_in, cast_out, cast_shapes = _cast_plumbing(cast_jobs, lambda i, n: i * n_steps + n)
    cast_steps = max([j.src.shape[1] // j.chunk_rows for j in cast_jobs], default=0)
    assert cast_steps <= (m // tm) * n_steps
    return pl.pallas_call(
        functools.partial(_inproj_kernel, tm=tm, n_aliased=len(aliases), n_casts=len(cast_jobs),
                          cast_steps=cast_steps),
        grid=(m // tm, n_steps),
        in_specs=in_specs + cast_in, out_specs=out_specs + tuple(cast_out), out_shape=out_shape + tuple(cast_shapes),
        scratch_shapes=[pltpu.VMEM((tm, D_MODEL), BF16), pltpu.VMEM((2, tm, PROJ_TN), F32)],
        input_output_aliases=aliases,
        compiler_params=_cparams(("arbitrary", "arbitrary")),
        name="inproj",
    )(*args, *[j.src for j in cast_jobs])


CUM_BLK = 256


def _cumsum_kernel(lf_ref, ccol_ref, crow_ref, *, s):
    r = lax.broadcasted_iota(jnp.int32, (CUM_BLK, CUM_BLK), 0)
    c = lax.broadcasted_iota(jnp.int32, (CUM_BLK, CUM_BLK), 1)
    tri = (c <= r).astype(BF16)
    carry = jnp.zeros((1, LANES), F32)
    for b in range(s // CUM_BLK):
        rows = slice(b * CUM_BLK, (b + 1) * CUM_BLK)
        lf = lf_ref[0, rows, :]
        pieces = _split3(lf)
        cs = jnp.dot(tri, pieces[0], preferred_element_type=F32)
        cs = cs + jnp.dot(tri, pieces[1], preferred_element_type=F32)
        cs = cs + jnp.dot(tri, pieces[2], preferred_element_type=F32)
        cs = cs + carry
        carry = cs[CUM_BLK - 1:CUM_BLK, :]
        cs2 = cs * LOG2E
        ccol_ref[0, rows, :] = cs2[:, :N_HEADS]
        crow_ref[0, :, rows] = cs2.T[:N_HEADS, :]


def _cumsum(logf_pad, nseq, s):
    lf3 = logf_pad.reshape(nseq, s, LANES)
    return pl.pallas_call(
        functools.partial(_cumsum_kernel, s=s),
        grid=(nseq,),
        in_specs=[pl.BlockSpec((1, s, LANES), lambda b: (b, 0, 0))],
        out_specs=(pl.BlockSpec((1, s, N_HEADS), lambda b: (b, 0, 0)),
                   pl.BlockSpec((1, N_HEADS, s), lambda b: (b, 0, 0))),
        out_shape=(jax.ShapeDtypeStruct((nseq, s, N_HEADS), F32),
                   jax.ShapeDtypeStruct((nseq, N_HEADS, s), F32)),
        compiler_params=_cparams(("arbitrary",)),
        name="logf_cumsum",
    )(lf3)


POOL_HALO = 16
CONV_HALO = 8


def _mix_kernel(u_ref, hb_ref, hc_ref, hx_ref, pw_ref, ps_ref, cw_ref, gp_ref, gc_ref,
                mix_ref, pstate_ref, cstate_ref, ubuf, zbuf, *, ts):
    si = pl.program_id(1)

    @pl.when(si == 0)
    def _():
        ubuf[0:POOL_HALO, :] = jnp.zeros((POOL_HALO, POOL_WIDTH), F32)
        zbuf[0:CONV_HALO, :] = jnp.zeros((CONV_HALO, CONV_WIDTH), F32)

    @pl.when(si > 0)
    def _():
        ubuf[0:POOL_HALO, :] = ubuf[ts:ts + POOL_HALO, :]
        zbuf[0:CONV_HALO, :] = zbuf[ts:ts + CONV_HALO, :]

    u = u_ref[...]
    ubuf[POOL_HALO:POOL_HALO + ts, :] = u
    z = hc_ref[...] * hx_ref[...]
    zbuf[CONV_HALO:CONV_HALO + ts, :] = z

    pos = si * ts + lax.broadcasted_iota(jnp.int32, (ts, 1), 0)
    ys = []
    for g, win in enumerate(POOL_WINDOWS):
        cols = slice(g * POOL_GROUP_DIM, (g + 1) * POOL_GROUP_DIM)
        ug = u[:, cols]
        acc = ug
        for j in range(1, win):
            acc = acc + ubuf[POOL_HALO - j:POOL_HALO - j + ts, cols]
        cnt = jnp.minimum(pos + 1, win).astype(F32)
        pooled = (acc / cnt - ug).astype(BF16)
        ys.append(jnp.dot(pooled, pw_ref[g], preferred_element_type=F32))
    y_pool = jnp.concatenate(ys, axis=1) * ps_ref[...]
    mix_ref[:, 0:POOL_WIDTH] = _rms(y_pool, gp_ref[...]).astype(BF16)

    conv = (cw_ref[0:1, :] * zbuf[CONV_HALO - 2:CONV_HALO - 2 + ts, :]
            + cw_ref[1:2, :] * zbuf[CONV_HALO - 1:CONV_HALO - 1 + ts, :]
            + cw_ref[2:3, :] * z)
    y_conv = hb_ref[...] * conv
    mix_ref[:, POOL_WIDTH:POOL_WIDTH + CONV_WIDTH] = _rms(y_conv, gc_ref[...]).astype(BF16)

    pstate_ref[0] = ubuf[POOL_HALO + ts - POOL_STATE:POOL_HALO + ts, :]
    cstate_ref[0] = zbuf[CONV_HALO + ts - (CONV_K - 1):CONV_HALO + ts, :]


def _mixers(proj, pool_w_b, pool_scale, conv_w, g_pool, g_conv, nseq, s, ts=512):
    m = proj.shape[0]
    nst = s // ts
    pcol = lambda j: pl.BlockSpec((ts, PROJ_TN), lambda b, i: (b * nst + i, j))
    const = lambda shape: pl.BlockSpec(shape, lambda b, i: (0,) * len(shape))
    return pl.pallas_call(
        functools.partial(_mix_kernel, ts=ts),
        grid=(nseq, nst),
        in_specs=[pcol(0), pcol(1), pcol(2), pcol(3),
                  const((len(POOL_WINDOWS), POOL_GROUP_DIM, POOL_GROUP_DIM)),
                  const((1, POOL_WIDTH)), const((CONV_K, CONV_WIDTH)),
                  const((1, POOL_WIDTH)), const((1, CONV_WIDTH))],
        out_specs=(pl.BlockSpec((ts, POOL_WIDTH + CONV_WIDTH), lambda b, i: (b * nst + i, 0)),
                   pl.BlockSpec((1, POOL_STATE, POOL_WIDTH), lambda b, i: (b, 0, 0)),
                   pl.BlockSpec((1, CONV_K - 1, CONV_WIDTH), lambda b, i: (b, 0, 0))),
        out_shape=(jax.ShapeDtypeStruct((m, POOL_WIDTH + CONV_WIDTH), BF16),
                   jax.ShapeDtypeStruct((nseq, POOL_STATE, POOL_WIDTH), F32),
                   jax.ShapeDtypeStruct((nseq, CONV_K - 1, CONV_WIDTH), F32)),
        scratch_shapes=[pltpu.VMEM((POOL_HALO + ts, POOL_WIDTH), F32),
                        pltpu.VMEM((CONV_HALO + ts, CONV_WIDTH), F32)],
        compiler_params=_cparams(("arbitrary", "arbitrary")),
        name="prompt_mixers",
    )(proj, proj, proj, proj, pool_w_b, pool_scale, conv_w, g_pool, g_conv)


def _fox_kernel(q_ref, k_ref, v_ref, cq_ref, ck_ref, g_ref, *rest, tq, tk, n_casts):
    cast_src = rest[:n_casts]
    out_ref = rest[n_casts]
    cast_dst = rest[n_casts + 1:2 * n_casts + 1]
    m_scr, l_scr, cq_scr, acc_scr = rest[2 * n_casts + 1:]
    _run_casts(cast_src, cast_dst)
    i = pl.program_id(1)
    causal = (lax.broadcasted_iota(jnp.int32, (tq, tq), 1) <= lax.broadcasted_iota(jnp.int32, (tq, tq), 0))

    for h in range(N_HEADS):
        m_scr[h] = jnp.full((tq, LANES), -jnp.inf, F32)
        l_scr[h] = jnp.zeros((tq, LANES), F32)
        cq_scr[h] = jnp.broadcast_to(cq_ref[0, :, h:h + 1], (tq, LANES))
    acc_scr[...] = jnp.zeros(acc_scr.shape, F32)

    def step(k0, width, masked):
        ks = pl.ds(pl.multiple_of(k0, width), width)
        lanes = lambda x: jnp.concatenate([x] * (width // LANES), axis=1)
        ones = jnp.ones((width, HEAD_DIM), BF16)
        for h in range(N_HEADS):
            cs = slice(h * HEAD_DIM, (h + 1) * HEAD_DIM)
            s = lax.dot_general(q_ref[:, cs], k_ref[ks, cs], (((1,), (1,)), ((), ())),
                                preferred_element_type=F32)
            s = s + lanes(cq_scr[h]) - ck_ref[0, h:h + 1, ks]
            if masked:
                s = jnp.where(causal, s, -jnp.inf)
            m_prev = m_scr[h]
            m_new = jnp.maximum(m_prev, jnp.broadcast_to(jnp.max(s, axis=1, keepdims=True), (tq, LANES)))
            alpha = jnp.exp2(m_prev - m_new)
            p = jnp.exp2(s - lanes(m_new)).astype(BF16)
            pv = jnp.dot(p, jnp.concatenate([v_ref[ks, cs], ones], axis=1), preferred_element_type=F32)
            l_scr[h] = alpha * l_scr[h] + pv[:, HEAD_DIM:]
            acc_scr[:, cs] = alpha * acc_scr[:, cs] + pv[:, :HEAD_DIM]
            m_scr[h] = m_new

    def body(j, c):
        step(j * tk, tk, False)
        return c

    n_wide = (i * tq) // tk
    lax.fori_loop(0, n_wide, body, 0)

    @pl.when(n_wide * tk < i * tq)
    def _():
        step(n_wide * tk, tq, False)

    step(i * tq, tq, True)

    for h in range(N_HEADS):
        cs = slice(h * HEAD_DIM, (h + 1) * HEAD_DIM)
        acc_scr[:, cs] = acc_scr[:, cs] / l_scr[h]
    out_ref[...] = _rms(acc_scr[...], g_ref[...]).astype(BF16)


def _fox_prompt(qb, kb, vb, ccol, crow, g_att, nseq, s, cast_jobs=(), tq=256, tk=512):
    assert tk == 2 * tq and s % tk == 0
    m = qb.shape[0]
    nq = s // tq
    stat = pltpu.VMEM((N_HEADS, tq, LANES), F32)
    cast_in, cast_out, cast_shapes = _cast_plumbing(cast_jobs, lambda b, i: b * nq + i)
    assert all(j.src.shape[1] // j.chunk_rows == nseq * nq for j in cast_jobs)
    return pl.pallas_call(
        functools.partial(_fox_kernel, tq=tq, tk=tk, n_casts=len(cast_jobs)),
        grid=(nseq, nq),
        in_specs=[pl.BlockSpec((tq, ATT_WIDTH), lambda b, i: (b * nq + i, 0)),
                  pl.BlockSpec((s, ATT_WIDTH), lambda b, i: (b, 0)),
                  pl.BlockSpec((s, ATT_WIDTH), lambda b, i: (b, 0)),
                  pl.BlockSpec((1, tq, N_HEADS), lambda b, i: (b, i, 0)),
                  pl.BlockSpec((1, N_HEADS, s), lambda b, i: (b, 0, 0)),
                  pl.BlockSpec((1, ATT_WIDTH), lambda b, i: (0, 0))] + cast_in,
        out_specs=[pl.BlockSpec((tq, ATT_WIDTH), lambda b, i: (b * nq + i, 0))] + cast_out,
        out_shape=[jax.ShapeDtypeStruct((m, ATT_WIDTH), BF16)] + cast_shapes,
        scratch_shapes=[stat, stat, stat, pltpu.VMEM((tq, ATT_WIDTH), F32)],
        compiler_params=_cparams(("arbitrary", "arbitrary")),
        name="fox_prompt",
    )(qb, kb, vb, ccol, crow, g_att, *[j.src for j in cast_jobs])


def _outproj_kernel(x_ref, ma_ref, mb_ref, wa_ref, wb_ref, o_ref):
    acc = jnp.dot(ma_ref[...], wa_ref[...], preferred_element_type=F32)
    acc = acc + jnp.dot(mb_ref[...], wb_ref[...], preferred_element_type=F32)
    o_ref[...] = x_ref[...] + acc


def _outproj(x, mix_a, mix_b, w_o_b, tm, tn=512):
    m = x.shape[0]
    half = D_MODEL // 2
    return pl.pallas_call(
        _outproj_kernel,
        grid=(m // tm, D_MODEL // tn),
        in_specs=[pl.BlockSpec((tm, tn), lambda i, n: (i, n)),
                  pl.BlockSpec((tm, half), lambda i, n: (i, 0)),
                  pl.BlockSpec((tm, half), lambda i, n: (i, 0)),
                  pl.BlockSpec((half, tn), lambda i, n: (0, n)),
                  pl.BlockSpec((half, tn), lambda i, n: (1, n))],
        out_specs=pl.BlockSpec((tm, tn), lambda i, n: (i, n)),
        out_shape=jax.ShapeDtypeStruct((m, D_MODEL), F32),
        compiler_params=_cparams(("arbitrary", "arbitrary")),
        name="outproj",
    )(x, mix_a, mix_b, w_o_b, w_o_b)


FFN_HALO = 8


def _ffn_kernel(*refs, tm, nf, tiles_per_seq, per_row_state):
    if per_row_state:
        (x_ref, g2_ref, wa_ref, wg_ref, cw_ref, cb_ref, wd_ref, prev_ref,
         o_ref, state_ref, xn_ref, act_scr, abuf, carry_ref) = refs
    else:
        (x_ref, g2_ref, wa_ref, wg_ref, cw_ref, cb_ref, wd_ref,
         o_ref, state_ref, xn_ref, act_scr, abuf, carry_ref) = refs
    mi = pl.program_id(0)
    f = pl.program_id(1)
    rc = min(tm, 256)

    @pl.when(f == 0)
    def _():
        def body(i, c):
            r = pl.ds(pl.multiple_of(i * rc, rc), rc)
            x = x_ref[r, :]
            xn_ref[r, :] = _rms(x, g2_ref[...]).astype(BF16)
            o_ref[r, :] = x
            return c
        lax.fori_loop(0, tm // rc, body, 0)

        @pl.when(mi == 0)
        def _():
            carry_ref[...] = jnp.zeros(carry_ref.shape, F32)

    def up():
        xn = xn_ref[...]
        a = jnp.dot(xn, wa_ref[...], preferred_element_type=F32)
        gate = jnp.dot(xn, wg_ref[...], preferred_element_type=F32)
        if per_row_state:
            p1 = prev_ref[1]
            ac = cw_ref[0:1, :] * prev_ref[0] + cw_ref[1:2, :] * p1 + cw_ref[2:3, :] * a
            state_ref[0] = p1
            state_ref[1] = a
        else:
            first = (mi % tiles_per_seq) == 0
            abuf[0:FFN_HALO, :] = jnp.where(first, 0.0, carry_ref[f])
            abuf[FFN_HALO:FFN_HALO + tm, :] = a
            carry_ref[f] = a[tm - FFN_HALO:tm, :]
            ac = (cw_ref[0:1, :] * abuf[FFN_HALO - 2:FFN_HALO - 2 + tm, :]
                  + cw_ref[1:2, :] * abuf[FFN_HALO - 1:FFN_HALO - 1 + tm, :]
                  + cw_ref[2:3, :] * a)
            state_ref[0] = a[tm - (FFN_K - 1):tm, :]
        ac = ac + cb_ref[...]
        act_scr[f % 2] = (ac * jax.nn.sigmoid(ac) * gate).astype(BF16)

    def down():
        o_ref[...] += jnp.dot(act_scr[(f + 1) % 2], wd_ref[...], preferred_element_type=F32)

    @pl.when(f == 0)
    def _():
        up()

    @pl.when((f > 0) & (f < nf))
    def _():
        down()
        up()

    @pl.when(f == nf)
    def _():
        down()


def _ffn(x, g2, w_up_b, cw, cb, w_down_b, prev, tm, seq_len, tf=512):
    m = x.shape[0]
    nf = D_FF // tf
    per_row_state = prev is not None
    cur = lambda f: jnp.minimum(f, nf - 1)
    old = lambda f: jnp.maximum(f - 1, 0)
    in_specs = [pl.BlockSpec((tm, D_MODEL), lambda i, f: (i, 0), pipeline_mode=pl.Buffered(1)),
                pl.BlockSpec((1, D_MODEL), lambda i, f: (0, 0)),
                pl.BlockSpec((D_MODEL, tf), lambda i, f: (0, cur(f))),
                pl.BlockSpec((D_MODEL, tf), lambda i, f: (0, nf + cur(f))),
                pl.BlockSpec((FFN_K, tf), lambda i, f: (0, cur(f))),
                pl.BlockSpec((1, tf), lambda i, f: (0, cur(f))),
                pl.BlockSpec((tf, D_MODEL), lambda i, f: (old(f), 0))]
    args = [x, g2, w_up_b, w_up_b, cw, cb, w_down_b]
    if per_row_state:
        tiles_per_seq = 1
        in_specs.append(pl.BlockSpec((FFN_K - 1, tm, tf), lambda i, f: (0, i, cur(f))))
        args.append(prev)
        state_spec = pl.BlockSpec((FFN_K - 1, tm, tf), lambda i, f: (0, i, cur(f)))
        state_shape = jax.ShapeDtypeStruct((FFN_K - 1, m, D_FF), F32)
    else:
        tiles_per_seq = seq_len // tm
        state_spec = pl.BlockSpec((1, FFN_K - 1, tf), lambda i, f: (i, 0, cur(f)))
        state_shape = jax.ShapeDtypeStruct((m // tm, FFN_K - 1, D_FF), F32)
    x_out, state = pl.pallas_call(
        functools.partial(_ffn_kernel, tm=tm, nf=nf, tiles_per_seq=tiles_per_seq, per_row_state=per_row_state),
        grid=(m // tm, nf + 1),
        in_specs=in_specs,
        out_specs=(pl.BlockSpec((tm, D_MODEL), lambda i, f: (i, 0)), state_spec),
        out_shape=(jax.ShapeDtypeStruct((m, D_MODEL), F32), state_shape),
        scratch_shapes=[pltpu.VMEM((tm, D_MODEL), BF16),
                        pltpu.VMEM((2, tm, tf), BF16),
                        pltpu.VMEM((FFN_HALO + tm, tf), F32),
                        pltpu.VMEM((nf, FFN_HALO, tf), F32)],
        compiler_params=_cparams(("arbitrary", "arbitrary")),
        name="ffn",
    )(*args)
    if not per_row_state:
        state = state[tiles_per_seq - 1::tiles_per_seq]
    return x_out, state


PAGE_ROW = PAGE_SIZE * N_HEADS


def _bias_kernel(pt_ref, pool_ref, lfnew_ref, bias_ref, lfbuf, *, db, n_pages):
    n_rows = db * n_pages

    def gather(r, c):
        lfbuf[pl.ds(r, 1), :] = pool_ref[pl.ds(pt_ref[r], 1), :]
        return c

    lax.fori_loop(0, n_rows, gather, 0)

    src = lax.broadcasted_iota(jnp.int32, (PAGE_ROW, PAGE_ROW), 0)
    dst = lax.broadcasted_iota(jnp.int32, (PAGE_ROW, PAGE_ROW), 1)
    same_head = (src & (N_HEADS - 1)) == (dst & (N_HEADS - 1))
    later = (src >> 3) > (dst >> 3)
    m_suffix = (same_head & later).astype(BF16)
    first = lax.broadcasted_iota(jnp.int32, (LANES, PAGE_ROW), 0)
    m_spread = (first == (lax.broadcasted_iota(jnp.int32, (LANES, PAGE_ROW), 1) & (N_HEADS - 1))).astype(BF16)
    pr = lax.broadcasted_iota(jnp.int32, (n_pages, n_pages), 0)
    pc = lax.broadcasted_iota(jnp.int32, (n_pages, n_pages), 1)
    later_page = (pc > pr).astype(BF16)

    lf = lfbuf[...]
    within = _dot01(_split3(lf), m_suffix)
    total = _dot01(_split3((within + lf)[:, :LANES]), m_spread)
    for b in range(db):
        rows = slice(b * n_pages, (b + 1) * n_pages)
        t0, t1, t2 = _split3(total[rows])
        after = jnp.dot(later_page, t0, preferred_element_type=F32)
        after = after + jnp.dot(later_page, t1, preferred_element_type=F32)
        after = after + jnp.dot(later_page, t2, preferred_element_type=F32)
        bias_ref[b] = within[rows] + after + lfnew_ref[b]


def _sample_bias(page_table, logf_flat, lfnew_rep):
    db, n_pages = page_table.shape
    grid_spec = pltpu.PrefetchScalarGridSpec(
        num_scalar_prefetch=1,
        grid=(1,),
        in_specs=[pl.BlockSpec(logf_flat.shape, lambda i, pt: (0, 0)),
                  pl.BlockSpec((db, 1, PAGE_ROW), lambda i, pt: (0, 0, 0))],
        out_specs=pl.BlockSpec((db, n_pages, PAGE_ROW), lambda i, pt: (0, 0, 0)),
        scratch_shapes=[pltpu.VMEM((db * n_pages, PAGE_ROW), F32)],
    )
    return pl.pallas_call(
        functools.partial(_bias_kernel, db=db, n_pages=n_pages),
        grid_spec=grid_spec,
        out_shape=jax.ShapeDtypeStruct((db, n_pages, PAGE_ROW), F32),
        compiler_params=_cparams(("arbitrary",)),
        name="sample_bias",
    )(page_table.reshape(db * n_pages), logf_flat, lfnew_rep)


def _head_allreduce(x, op, reduce_op):
    x = jnp.broadcast_to(reduce_op(x, axis=0, keepdims=True), (SUBLANES, LANES))
    for sh in (8, 16, 32, 64):
        x = op(x, pltpu.roll(x, sh, axis=1))
    return x


def _paged_kernel(pt_ref, q_ref, bias_ref, knew_ref, vnew_ref, *refs, n_steps):
    k_refs = refs[:PAGES_PER_STEP]
    v_refs = refs[PAGES_PER_STEP:2 * PAGES_PER_STEP]
    o_ref = refs[2 * PAGES_PER_STEP]
    m_scr, l_scr, acc_scr, s_scr = refs[2 * PAGES_PER_STEP + 1:]
    i = pl.program_id(1)

    @pl.when(i == 0)
    def _():
        m_scr[...] = jnp.full((SUBLANES, LANES), -jnp.inf, F32)
        l_scr[...] = jnp.zeros((SUBLANES, LANES), F32)
        acc_scr[...] = jnp.zeros((N_HEADS, HEAD_DIM), F32)

    q = q_ref[0]
    sub = lax.broadcasted_iota(jnp.int32, (N_HEADS, PAGE_ROW), 0)
    lane = lax.broadcasted_iota(jnp.int32, (N_HEADS, PAGE_ROW), 1)
    own_head = sub == (lane & (N_HEADS - 1))

    for j in range(PAGES_PER_STEP):
        k2 = k_refs[j][...].reshape(PAGE_ROW, HEAD_DIM).astype(BF16)
        st = lax.dot_general(q, k2, (((1,), (1,)), ((), ())), preferred_element_type=F32)
        s_scr[j:j + 1, :] = jnp.sum(jnp.where(own_head, st, 0.0), axis=0, keepdims=True)

    s = s_scr[...] + bias_ref[0] * LOG2E
    chunks = [s[:, c * LANES:(c + 1) * LANES] for c in range(PAGE_ROW // LANES)]
    mx = chunks[0]
    for c in chunks[1:]:
        mx = jnp.maximum(mx, c)
    m_prev = m_scr[...]
    m_new = jnp.maximum(m_prev, _head_allreduce(mx, jnp.maximum, jnp.max))
    alpha = jnp.exp2(m_prev - m_new)
    p = jnp.exp2(s - jnp.concatenate([m_new[0:1, :]] * (PAGE_ROW // LANES), axis=1))
    ps = p[:, 0:LANES]
    for c in range(1, PAGE_ROW // LANES):
        ps = ps + p[:, c * LANES:(c + 1) * LANES]
    l_scr[...] = alpha * l_scr[...] + _head_allreduce(ps, jnp.add, jnp.sum)
    m_scr[...] = m_new

    o = jnp.zeros((N_HEADS, HEAD_DIM), F32)
    for j in range(PAGES_PER_STEP):
        pj = jnp.where(own_head, jnp.broadcast_to(p[j:j + 1, :], (N_HEADS, PAGE_ROW)), 0.0).astype(BF16)
        v2 = v_refs[j][...].reshape(PAGE_ROW, HEAD_DIM).astype(BF16)
        o = o + jnp.dot(pj, v2, preferred_element_type=F32)

    sub8 = lax.broadcasted_iota(jnp.int32, (SUBLANES, LANES), 0)
    lane8 = lax.broadcasted_iota(jnp.int32, (SUBLANES, LANES), 1)
    diag = sub8 == lane8

    def to_col(x):
        return jnp.sum(jnp.where(diag, x, 0.0), axis=1, keepdims=True)

    acc = to_col(alpha) * acc_scr[...] + o
    acc_scr[...] = acc

    @pl.when(i == n_steps - 1)
    def _():
        m_col = to_col(m_new)
        l_col = to_col(l_scr[...])
        qf = q.astype(F32)
        kn = knew_ref[0].astype(BF16).astype(F32)
        s_new = jnp.sum(qf * kn, axis=1, keepdims=True)
        m_f = jnp.maximum(m_col, s_new)
        a_old = jnp.exp2(m_col - m_f)
        p_new = jnp.exp2(s_new - m_f)
        o_ref[0] = (acc * a_old + p_new * vnew_ref[0]) / (l_col * a_old + p_new)


def _paged_attention(layer, page_table, q3, bias, k_new3, v_new3, cache_k, cache_v):
    db, n_pages = page_table.shape
    n_steps = n_pages // PAGES_PER_STEP

    def page_spec(j):
        return pl.BlockSpec((None, None, PAGE_SIZE, N_HEADS, HEAD_DIM),
                            lambda b, i, pt: (layer, pt[b, i * PAGES_PER_STEP + j], 0, 0, 0))

    row = pl.BlockSpec((1, N_HEADS, HEAD_DIM), lambda b, i, pt: (b, 0, 0))
    grid_spec = pltpu.PrefetchScalarGridSpec(
        num_scalar_prefetch=1,
        grid=(db, n_steps),
        in_specs=[row,
                  pl.BlockSpec((1, PAGES_PER_STEP, PAGE_ROW), lambda b, i, pt: (b, i, 0)),
                  row, row]
                 + [page_spec(j) for j in range(PAGES_PER_STEP)]
                 + [page_spec(j) for j in range(PAGES_PER_STEP)],
        out_specs=row,
        scratch_shapes=[pltpu.VMEM((SUBLANES, LANES), F32), pltpu.VMEM((SUBLANES, LANES), F32),
                        pltpu.VMEM((N_HEADS, HEAD_DIM), F32), pltpu.VMEM((PAGES_PER_STEP, PAGE_ROW), F32)],
    )
    return pl.pallas_call(
        functools.partial(_paged_kernel, n_steps=n_steps),
        grid_spec=grid_spec,
        out_shape=jax.ShapeDtypeStruct((db, N_HEADS, HEAD_DIM), F32),
        compiler_params=_cparams(("arbitrary", "arbitrary")),
        name="paged_attention",
    )(page_table, q3, bias, k_new3, v_new3, *([cache_k] * PAGES_PER_STEP), *([cache_v] * PAGES_PER_STEP))


def _sample_mix_kernel(proj_ref, sp_ref, sc_ref, yatt_ref, pw_ref, ps_ref, cw_ref, gp_ref, gc_ref, ga_ref,
                       mixa_ref, mixb_ref, pstate_ref, cstate_ref):
    u = proj_ref[:, 0:POOL_WIDTH]
    hb = proj_ref[:, POOL_WIDTH:POOL_WIDTH + CONV_WIDTH]
    hc = proj_ref[:, POOL_WIDTH + CONV_WIDTH:POOL_WIDTH + 2 * CONV_WIDTH]
    hx = proj_ref[:, POOL_WIDTH + 2 * CONV_WIDTH:POOL_WIDTH + 3 * CONV_WIDTH]

    ys = []
    for g, win in enumerate(POOL_WINDOWS):
        cols = slice(g * POOL_GROUP_DIM, (g + 1) * POOL_GROUP_DIM)
        ug = u[:, cols]
        acc = ug
        for j in range(1, win):
            acc = acc + sp_ref[POOL_STATE - j, :, cols]
        cnt = float(min(POOL_STATE + 1, win))
        pooled = (acc / cnt - ug).astype(BF16)
        ys.append(jnp.dot(pooled, pw_ref[g], preferred_element_type=F32))
    y_pool = jnp.concatenate(ys, axis=1) * ps_ref[...]
    mixa_ref[:, 0:POOL_WIDTH] = _rms(y_pool, gp_ref[...]).astype(BF16)

    z = hc * hx
    conv = cw_ref[0:1, :] * sc_ref[0] + cw_ref[1:2, :] * sc_ref[1] + cw_ref[2:3, :] * z
    mixa_ref[:, POOL_WIDTH:POOL_WIDTH + CONV_WIDTH] = _rms(hb * conv, gc_ref[...]).astype(BF16)
    mixb_ref[...] = _rms(yatt_ref[...], ga_ref[...]).astype(BF16)

    for r in range(POOL_STATE - 1):
        pstate_ref[r] = sp_ref[r + 1]
    pstate_ref[POOL_STATE - 1] = u
    cstate_ref[0] = sc_ref[1]
    cstate_ref[1] = z


def _sample_mixers(proj, sp_t, sc_t, y_att, pool_w_b, pool_scale, conv_w, g_pool, g_conv, g_att):
    db = proj.shape[0]
    return pl.pallas_call(
        _sample_mix_kernel,
        out_shape=(jax.ShapeDtypeStruct((db, POOL_WIDTH + CONV_WIDTH), BF16),
                   jax.ShapeDtypeStruct((db, ATT_WIDTH), BF16),
                   jax.ShapeDtypeStruct((POOL_STATE, db, POOL_WIDTH), F32),
                   jax.ShapeDtypeStruct((CONV_K - 1, db, CONV_WIDTH), F32)),
        compiler_params=pltpu.CompilerParams(vmem_limit_bytes=VMEM_LIMIT),
        name="sample_mixers",
    )(proj, sp_t, sc_t, y_att, pool_w_b, pool_scale, conv_w, g_pool, g_conv, g_att)


def kernel(x_prompt, x_sample, cache_k, cache_v, cache_logf, state_pool, state_conv, state_ffn, page_table,
           norm1_g, w_in, b_f, pool_w, pool_scale, conv_w, q_norm_g, k_norm_g, out_norm_g, w_o, norm2_g,
           w_up, ffn_conv_w, ffn_conv_b, w_down):
    nb, s, _ = x_prompt.shape
    db = x_sample.shape[0]
    depth = w_in.shape[0]
    n_pool = cache_k.shape[1]

    xp = x_prompt.reshape(nb * s, D_MODEL)
    xs = x_sample.reshape(db, D_MODEL)
    logf_flat = cache_logf.reshape(depth, n_pool, PAGE_ROW)

    w_in_b_all = jnp.pad(w_in, ((0, 0), (0, 0), (0, D_PROJ + LANES - w_in.shape[2]))).astype(BF16)
    n_fox_steps = nb * (s // FOX_TQ)

    outs = {name: [] for name in ("fp", "pp", "cp", "ffp", "fs", "ps", "cs", "ffs")}
    kv_prompt = (jnp.zeros((depth, nb * s, ATT_WIDTH), F32), jnp.ones((depth, nb * s, ATT_WIDTH), F32))
    kv_sample = (jnp.zeros((depth, db, ATT_WIDTH), F32), jnp.ones((depth, db, ATT_WIDTH), F32))
    for l in range(depth):
        bf_pad = jnp.pad(b_f[l], (0, LANES - N_HEADS)).reshape(1, LANES)
        pool_w_b = pool_w[l].astype(BF16)
        g1 = norm1_g[l].reshape(1, D_MODEL)
        g2 = norm2_g[l].reshape(1, D_MODEL)
        qg = q_norm_g[l].reshape(1, HEAD_DIM)
        kg = k_norm_g[l].reshape(1, HEAD_DIM)
        ps_l = pool_scale[l].reshape(1, POOL_WIDTH)
        g_pool = out_norm_g[l, :POOL_WIDTH].reshape(1, POOL_WIDTH)
        g_conv = out_norm_g[l, POOL_WIDTH:POOL_WIDTH + CONV_WIDTH].reshape(1, CONV_WIDTH)
        g_att = out_norm_g[l, POOL_WIDTH + CONV_WIDTH:].reshape(1, ATT_WIDTH)
        cb = ffn_conv_b[l].reshape(1, D_FF)

        w_in_b = w_in_b_all[l]
        proj, qb, kp_stack, kb, vp_stack, vb, logf, logf_pad = _inproj(
            l, xp, g1, w_in_b, bf_pad, qg, kg, tm=1024, kv_stack=kv_prompt)
        kv_prompt = (kp_stack, vp_stack)
        ccol, crow = _cumsum(logf_pad, nb, s)
        mix_a, pstate, cstate = _mixers(proj, pool_w_b, ps_l, conv_w[l], g_pool, g_conv, nb, s)
        own_weights = [_CastJob(w_o, l, D_MODEL // n_fox_steps, D_MODEL),
                       _CastJob(w_up, l, D_MODEL // n_fox_steps, 2 * D_FF),
                       _CastJob(w_down, l, D_FF // n_fox_steps, D_MODEL)]
        mix_b, w_o_b, w_up_b, w_down_b = _fox_prompt(qb, kb, vb, ccol, crow, g_att, nb, s, cast_jobs=own_weights,
                                                     tq=FOX_TQ, tk=2 * FOX_TQ)
        x_mid = _outproj(xp, mix_a, mix_b, w_o_b, tm=1024)
        xp, fstate = _ffn(x_mid, g2, w_up_b, ffn_conv_w[l], cb, w_down_b, None, tm=PROMPT_FFN_TM, seq_len=s)
        outs["fp"].append(logf.reshape(nb, s, N_HEADS))
        outs["pp"].append(pstate)
        outs["cp"].append(cstate)
        outs["ffp"].append(fstate)

        proj_s, qb_s, ks_stack, _, vs_stack, _, logf_s, _ = _inproj(
            l, xs, g1, w_in_b, bf_pad, qg, kg, tm=db, kv_stack=kv_sample)
        kv_sample = (ks_stack, vs_stack)
        k32_s, v32_s = ks_stack[l], vs_stack[l]
        lfnew_rep = jnp.tile(logf_s, (1, PAGE_SIZE)).reshape(db, 1, PAGE_ROW)
        bias = _sample_bias(page_table, logf_flat[l], lfnew_rep)
        y_att = _paged_attention(l, page_table, qb_s.reshape(db, N_HEADS, HEAD_DIM), bias,
                                 k32_s.reshape(db, N_HEADS, HEAD_DIM), v32_s.reshape(db, N_HEADS, HEAD_DIM),
                                 cache_k, cache_v)
        mix_a_s, mix_b_s, pstate_t, cstate_t = _sample_mixers(
            proj_s, jnp.swapaxes(state_pool[l], 0, 1), jnp.swapaxes(state_conv[l], 0, 1),
            y_att.reshape(db, ATT_WIDTH), pool_w_b, ps_l, conv_w[l], g_pool, g_conv, g_att)
        xs_mid = _outproj(xs, mix_a_s, mix_b_s, w_o_b, tm=db)
        xs, fstate_t = _ffn(xs_mid, g2, w_up_b, ffn_conv_w[l], cb, w_down_b,
                            jnp.swapaxes(state_ffn[l], 0, 1), tm=db, seq_len=1)
        outs["fs"].append(logf_s.reshape(db, 1, N_HEADS))
        outs["ps"].append(jnp.swapaxes(pstate_t, 0, 1))
        outs["cs"].append(jnp.swapaxes(cstate_t, 0, 1))
        outs["ffs"].append(jnp.swapaxes(fstate_t, 0, 1))

    st = lambda name: jnp.stack(outs[name])
    return (xp.reshape(nb, s, D_MODEL), xs.reshape(db, 1, D_MODEL),
            kv_prompt[0].reshape(depth, nb, s, N_HEADS, HEAD_DIM), kv_prompt[1].reshape(depth, nb, s, N_HEADS, HEAD_DIM),
            st("fp"), st("pp"), st("cp"), st("ffp"),
            kv_sample[0].reshape(depth, db, 1, N_HEADS, HEAD_DIM), kv_sample[1].reshape(depth, db, 1, N_HEADS, HEAD_DIM),
            st("fs"), st("ps"), st("cs"), st("ffs"))
```

```python
import functools
from typing import NamedTuple

import jax
import jax.numpy as jnp
from jax import lax
from jax.experimental import pallas as pl
from jax.experimental.pallas import tpu as pltpu

F32 = jnp.float32
BF16 = jnp.bfloat16

D_MODEL = 2048
PAGE_SIZE = 128
POOL_WIDTH = 512
POOL_WINDOWS = (2, 4, 8, 16)
POOL_GROUP_DIM = 128
POOL_STATE = 15
CONV_WIDTH = 512
CONV_K = 3
ATT_WIDTH = 1024
HEAD_DIM = 128
N_HEADS = 8
D_FF = 5632
FFN_K = 3
EPS = 1e-6
D_PROJ = POOL_WIDTH + 3 * CONV_WIDTH + 3 * ATT_WIDTH
SCALE = HEAD_DIM ** -0.5

LANES = 128
SUBLANES = 8
VMEM_LIMIT = 56 * 1024 * 1024

PROJ_TN = 512
PROMPT_FFN_TM = 1024
FOX_TQ = 256
LOG2E = 1.4426950408889634

T_Q = (POOL_WIDTH + 3 * CONV_WIDTH) // PROJ_TN
T_K = T_Q + ATT_WIDTH // PROJ_TN
T_V = T_K + ATT_WIDTH // PROJ_TN
T_END = D_PROJ // PROJ_TN
PAGES_PER_STEP = 16


def _cparams(sem):
    return pltpu.CompilerParams(dimension_semantics=sem, vmem_limit_bytes=VMEM_LIMIT)


def _rms(x, g):
    ms = jnp.mean(x * x, axis=-1, keepdims=True)
    return x * lax.rsqrt(ms + EPS) * g


def _log_sigmoid(z):
    return jnp.minimum(z, 0.0) - jnp.log1p(jnp.exp(-jnp.abs(z)))


def _split3(x):
    p0 = x.astype(BF16)
    r = x - p0.astype(F32)
    p1 = r.astype(BF16)
    p2 = (r - p1.astype(F32)).astype(BF16)
    return p0, p1, p2


def _dot01(a_pieces, b):
    out = jnp.dot(a_pieces[0], b, preferred_element_type=F32)
    for p in a_pieces[1:]:
        out = out + jnp.dot(p, b, preferred_element_type=F32)
    return out


class _CastJob(NamedTuple):
    src: jax.Array
    layer: int
    chunk_rows: int
    out_cols: int


def _cast_plumbing(jobs, step_of):
    in_specs, out_specs, out_shapes = [], [], []
    for job in jobs:
        _, rows, cols = job.src.shape
        n_chunks = rows // job.chunk_rows
        chunk = lambda *g, n_chunks=n_chunks: jnp.minimum(step_of(*g), n_chunks - 1)
        in_specs.append(pl.BlockSpec((None, job.chunk_rows, cols),
                                     lambda *g, job=job, chunk=chunk: (job.layer, chunk(*g), 0)))
        out_specs.append(pl.BlockSpec((job.chunk_rows, job.out_cols), lambda *g, chunk=chunk: (chunk(*g), 0)))
        out_shapes.append(jax.ShapeDtypeStruct((rows, job.out_cols), BF16))
    return in_specs, out_specs, out_shapes


def _run_casts(src_refs, dst_refs):
    for src, dst in zip(src_refs, dst_refs):
        cols = src.shape[1]
        if dst.shape[1] > cols:
            dst[...] = jnp.zeros(dst.shape, BF16)
        dst[:, :cols] = src[...].astype(BF16)


def _inproj_kernel(x_ref, g1_ref, w_ref, wf_ref, bf_ref, qg_ref, kg_ref, *rest, tm, n_aliased, n_casts, cast_steps):
    cast_src = rest[n_aliased:n_aliased + n_casts]
    proj_ref, q_ref, k32_ref, kb_ref, v32_ref, vb_ref, logf_ref, logfp_ref = rest[n_aliased + n_casts:][:8]
    cast_dst = rest[n_aliased + n_casts + 8:][:n_casts]
    xn_ref, y_scr = rest[-2:]
    n = pl.program_id(1)
    rc = min(tm, 256)

    if n_casts:
        @pl.when(pl.program_id(0) * pl.num_programs(1) + n < cast_steps)
        def _():
            _run_casts(cast_src, cast_dst)

    def normalise():
        def body(i, c):
            r = pl.ds(pl.multiple_of(i * rc, rc), rc)
            xn_ref[r, :] = _rms(x_ref[r, :], g1_ref[...]).astype(BF16)
            return c
        lax.fori_loop(0, tm // rc, body, 0)

    def matmul():
        return jnp.dot(xn_ref[...], w_ref[...], preferred_element_type=F32)

    def park():
        y_scr[n % 2] = matmul()

    def head_norm(g):
        y = y_scr[(n + 1) % 2]
        parts = []
        for h in range(PROJ_TN // HEAD_DIM):
            parts.append(_rms(y[:, h * HEAD_DIM:(h + 1) * HEAD_DIM], g))
        return jnp.concatenate(parts, axis=1)

    def finish_q():
        q_ref[...] = (head_norm(qg_ref[...]) * (SCALE * LOG2E)).astype(BF16)

    def finish_k():
        kn = head_norm(kg_ref[...])
        k32_ref[...] = kn
        kb_ref[...] = kn.astype(BF16)

    def direct_v():
        y = matmul()
        v32_ref[...] = y
        vb_ref[...] = y.astype(BF16)

    @pl.when(n == 0)
    def _():
        normalise()
        fl = jnp.dot(xn_ref[...], wf_ref[...], preferred_element_type=F32)
        lf = _log_sigmoid(fl + bf_ref[...])
        lane = lax.broadcasted_iota(jnp.int32, lf.shape, 1)
        lf = jnp.where(lane < N_HEADS, lf, 0.0)
        logfp_ref[...] = lf
        logf_ref[...] = lf[:, :N_HEADS]

    @pl.when(n < T_Q)
    def _():
        proj_ref[...] = matmul()

    @pl.when(n == T_Q)
    def _():
        park()

    @pl.when((n > T_Q) & (n <= T_K))
    def _():
        finish_q()
        park()

    @pl.when((n > T_K) & (n < T_V))
    def _():
        finish_k()
        park()

    @pl.when(n == T_V)
    def _():
        finish_k()
        direct_v()

    @pl.when(n > T_V)
    def _():
        direct_v()


def _inproj(layer, x, g1, w_in_b, bf_pad, qg, kg, tm, kv_stack, cast_jobs=()):
    m = x.shape[0]
    depth = kv_stack[0].shape[0]
    n_steps = T_END
    col = lambda first_step: (lambda i, n: (i, jnp.clip(n - first_step, 0, 1)))
    slab = lambda first_step: (lambda i, n: (layer, i, jnp.clip(n - first_step, 0, 1)))
    out_shape = (
        jax.ShapeDtypeStruct((m, 4 * PROJ_TN), F32),
        jax.ShapeDtypeStruct((m, ATT_WIDTH), BF16),
        jax.ShapeDtypeStruct((depth, m, ATT_WIDTH), F32),
        jax.ShapeDtypeStruct((m, ATT_WIDTH), BF16),
        jax.ShapeDtypeStruct((depth, m, ATT_WIDTH), F32),
        jax.ShapeDtypeStruct((m, ATT_WIDTH), BF16),
        jax.ShapeDtypeStruct((m, N_HEADS), F32),
        jax.ShapeDtypeStruct((m, LANES), F32),
    )
    out_specs = (
        pl.BlockSpec((tm, PROJ_TN), lambda i, n: (i, jnp.minimum(n, T_Q - 1))),
        pl.BlockSpec((tm, PROJ_TN), col(T_Q + 1)),
        pl.BlockSpec((None, tm, PROJ_TN), slab(T_K + 1)),
        pl.BlockSpec((tm, PROJ_TN), col(T_K + 1)),
        pl.BlockSpec((None, tm, PROJ_TN), slab(T_V)),
        pl.BlockSpec((tm, PROJ_TN), col(T_V)),
        pl.BlockSpec((tm, N_HEADS), lambda i, n: (i, 0)),
        pl.BlockSpec((tm, LANES), lambda i, n: (i, 0)),
    )
    in_specs = [
        pl.BlockSpec((tm, D_MODEL), lambda i, n: (i, 0)),
        pl.BlockSpec((1, D_MODEL), lambda i, n: (0, 0)),
        pl.BlockSpec((None, D_MODEL, PROJ_TN), lambda i, n: (layer, 0, n)),
        pl.BlockSpec((None, D_MODEL, LANES), lambda i, n: (layer, 0, D_PROJ // LANES)),
        pl.BlockSpec((1, LANES), lambda i, n: (0, 0)),
        pl.BlockSpec((1, HEAD_DIM), lambda i, n: (0, 0)),
        pl.BlockSpec((1, HEAD_DIM), lambda i, n: (0, 0)),
        pl.BlockSpec(memory_space=pl.ANY), pl.BlockSpec(memory_space=pl.ANY),
    ]
    args = [x, g1, w_in_b, w_in_b, bf_pad, qg, kg, *kv_stack]
    aliases = {7: 2, 8: 4}
    cast_in, cast_out, cast_shapes = _cast_plumbing(cast_jobs, lambda i, n: i * n_steps + n)
    cast_steps = max([j.src.shape[1] // j.chunk_rows for j in cast_jobs], default=0)
    assert cast_steps <= (m // tm) * n_steps
    return pl.pallas_call(
        functools.partial(_inproj_kernel, tm=tm, n_aliased=len(aliases), n_casts=len(cast_jobs),
                          cast_steps=cast_steps),
        grid=(m // tm, n_steps),
        in_specs=in_specs + cast_in, out_specs=out_specs + tuple(cast_out), out_shape=out_shape + tuple(cast_shapes),
        scratch_shapes=[pltpu.VMEM((tm, D_MODEL), BF16), pltpu.VMEM((2, tm, PROJ_TN), F32)],
        input_output_aliases=aliases,
        compiler_params=_cparams(("arbitrary", "arbitrary")),
        name="inproj",
    )(*args, *[j.src for j in cast_jobs])


CUM_BLK = 256


def _cumsum_kernel(lf_ref, ccol_ref, crow_ref, *, s):
    r = lax.broadcasted_iota(jnp.int32, (CUM_BLK, CUM_BLK), 0)
    c = lax.broadcasted_iota(jnp.int32, (CUM_BLK, CUM_BLK), 1)
    tri = (c <= r).astype(BF16)
    carry = jnp.zeros((1, LANES), F32)
    for b in range(s // CUM_BLK):
        rows = slice(b * CUM_BLK, (b + 1) * CUM_BLK)
        lf = lf_ref[0, rows, :]
        pieces = _split3(lf)
        cs = jnp.dot(tri, pieces[0], preferred_element_type=F32)
        cs = cs + jnp.dot(tri, pieces[1], preferred_element_type=F32)
        cs = cs + jnp.dot(tri, pieces[2], preferred_element_type=F32)
        cs = cs + carry
        carry = cs[CUM_BLK - 1:CUM_BLK, :]
        cs2 = cs * LOG2E
        ccol_ref[0, rows, :] = cs2[:, :N_HEADS]
        crow_ref[0, :, rows] = cs2.T[:N_HEADS, :]


def _cumsum(logf_pad, nseq, s):
    lf3 = logf_pad.reshape(nseq, s, LANES)
    return pl.pallas_call(
        functools.partial(_cumsum_kernel, s=s),
        grid=(nseq,),
        in_specs=[pl.BlockSpec((1, s, LANES), lambda b: (b, 0, 0))],
        out_specs=(pl.BlockSpec((1, s, N_HEADS), lambda b: (b, 0, 0)),
                   pl.BlockSpec((1, N_HEADS, s), lambda b: (b, 0, 0))),
        out_shape=(jax.ShapeDtypeStruct((nseq, s, N_HEADS), F32),
                   jax.ShapeDtypeStruct((nseq, N_HEADS, s), F32)),
        compiler_params=_cparams(("arbitrary",)),
        name="logf_cumsum",
    )(lf3)


POOL_HALO = 16
CONV_HALO = 8


def _mix_kernel(u_ref, hb_ref, hc_ref, hx_ref, pw_ref, ps_ref, cw_ref, gp_ref, gc_ref,
                mix_ref, pstate_ref, cstate_ref, ubuf, zbuf, *, ts):
    si = pl.program_id(1)

    @pl.when(si == 0)
    def _():
        ubuf[0:POOL_HALO, :] = jnp.zeros((POOL_HALO, POOL_WIDTH), F32)
        zbuf[0:CONV_HALO, :] = jnp.zeros((CONV_HALO, CONV_WIDTH), F32)

    @pl.when(si > 0)
    def _():
        ubuf[0:POOL_HALO, :] = ubuf[ts:ts + POOL_HALO, :]
        zbuf[0:CONV_HALO, :] = zbuf[ts:ts + CONV_HALO, :]

    u = u_ref[...]
    ubuf[POOL_HALO:POOL_HALO + ts, :] = u
    z = hc_ref[...] * hx_ref[...]
    zbuf[CONV_HALO:CONV_HALO + ts, :] = z

    pos = si * ts + lax.broadcasted_iota(jnp.int32, (ts, 1), 0)
    ys = []
    for g, win in enumerate(POOL_WINDOWS):
        cols = slice(g * POOL_GROUP_DIM, (g + 1) * POOL_GROUP_DIM)
        ug = u[:, cols]
        acc = ug
        for j in range(1, win):
            acc = acc + ubuf[POOL_HALO - j:POOL_HALO - j + ts, cols]
        cnt = jnp.minimum(pos + 1, win).astype(F32)
        pooled = (acc / cnt - ug).astype(BF16)
        ys.append(jnp.dot(pooled, pw_ref[g], preferred_element_type=F32))
    y_pool = jnp.concatenate(ys, axis=1) * ps_ref[...]
    mix_ref[:, 0:POOL_WIDTH] = _rms(y_pool, gp_ref[...]).astype(BF16)

    conv = (cw_ref[0:1, :] * zbuf[CONV_HALO - 2:CONV_HALO - 2 + ts, :]
            + cw_ref[1:2, :] * zbuf[CONV_HALO - 1:CONV_HALO - 1 + ts, :]
            + cw_ref[2:3, :] * z)
    y_conv = hb_ref[...] * conv
    mix_ref[:, POOL_WIDTH:POOL_WIDTH + CONV_WIDTH] = _rms(y_conv, gc_ref[...]).astype(BF16)

    pstate_ref[0] = ubuf[POOL_HALO + ts - POOL_STATE:POOL_HALO + ts, :]
    cstate_ref[0] = zbuf[CONV_HALO + ts - (CONV_K - 1):CONV_HALO + ts, :]


def _mixers(proj, pool_w_b, pool_scale, conv_w, g_pool, g_conv, nseq, s, ts=512):
    m = proj.shape[0]
    nst = s // ts
    pcol = lambda j: pl.BlockSpec((ts, PROJ_TN), lambda b, i: (b * nst + i, j))
    const = lambda shape: pl.BlockSpec(shape, lambda b, i: (0,) * len(shape))
    return pl.pallas_call(
        functools.partial(_mix_kernel, ts=ts),
        grid=(nseq, nst),
        in_specs=[pcol(0), pcol(1), pcol(2), pcol(3),
                  const((len(POOL_WINDOWS), POOL_GROUP_DIM, POOL_GROUP_DIM)),
                  const((1, POOL_WIDTH)), const((CONV_K, CONV_WIDTH)),
                  const((1, POOL_WIDTH)), const((1, CONV_WIDTH))],
        out_specs=(pl.BlockSpec((ts, POOL_WIDTH + CONV_WIDTH), lambda b, i: (b * nst + i, 0)),
                   pl.BlockSpec((1, POOL_STATE, POOL_WIDTH), lambda b, i: (b, 0, 0)),
                   pl.BlockSpec((1, CONV_K - 1, CONV_WIDTH), lambda b, i: (b, 0, 0))),
        out_shape=(jax.ShapeDtypeStruct((m, POOL_WIDTH + CONV_WIDTH), BF16),
                   jax.ShapeDtypeStruct((nseq, POOL_STATE, POOL_WIDTH), F32),
                   jax.ShapeDtypeStruct((nseq, CONV_K - 1, CONV_WIDTH), F32)),
        scratch_shapes=[pltpu.VMEM((POOL_HALO + ts, POOL_WIDTH), F32),
                        pltpu.VMEM((CONV_HALO + ts, CONV_WIDTH), F32)],
        compiler_params=_cparams(("arbitrary", "arbitrary")),
        name="prompt_mixers",
    )(proj, proj, proj, proj, pool_w_b, pool_scale, conv_w, g_pool, g_conv)


def _fox_kernel(q_ref, k_ref, v_ref, cq_ref, ck_ref, g_ref, *rest, tq, tk, nq, n_casts):
    cast_src = rest[:n_casts]
    out_ref = rest[n_casts]
    cast_dst = rest[n_casts + 1:2 * n_casts + 1]
    m_scr, l_scr, cq_scr, acc_scr = rest[2 * n_casts + 1:]
    _run_casts(cast_src, cast_dst)
    i = pl.program_id(1)
    causal = (lax.broadcasted_iota(jnp.int32, (tq, tq), 1) <= lax.broadcasted_iota(jnp.int32, (tq, tq), 0))

    for h in range(N_HEADS):
        m_scr[h] = jnp.full((tq, LANES), -jnp.inf, F32)
        l_scr[h] = jnp.zeros((tq, LANES), F32)
        cq_scr[h] = jnp.broadcast_to(cq_ref[0, :, h:h + 1], (tq, LANES))
    acc_scr[...] = jnp.zeros(acc_scr.shape, F32)

    def step(k0, width, masked):
        ks = pl.ds(k0, width)
        lanes = lambda x: jnp.concatenate([x] * (width // LANES), axis=1)
        ones = jnp.ones((width, HEAD_DIM), BF16)
        for h in range(N_HEADS):
            cs = slice(h * HEAD_DIM, (h + 1) * HEAD_DIM)
            s = lax.dot_general(q_ref[:, cs], k_ref[ks, cs], (((1,), (1,)), ((), ())),
                                preferred_element_type=F32)
            s = s + lanes(cq_scr[h]) - ck_ref[0, h:h + 1, ks]
            if masked:
                s = jnp.where(causal, s, -jnp.inf)
            m_prev = m_scr[h]
            m_new = jnp.maximum(m_prev, jnp.broadcast_to(jnp.max(s, axis=1, keepdims=True), (tq, LANES)))
            alpha = jnp.exp2(m_prev - m_new)
            p = jnp.exp2(s - lanes(m_new)).astype(BF16)
            pv = jnp.dot(p, jnp.concatenate([v_ref[ks, cs], ones], axis=1), preferred_element_type=F32)
            l_scr[h] = alpha * l_scr[h] + pv[:, HEAD_DIM:]
            acc_scr[:, cs] = alpha * acc_scr[:, cs] + pv[:, :HEAD_DIM]
            m_scr[h] = m_new

    for c in range(nq):
        @pl.when(i == c)
        def _(c=c):
            n_wide = (c * tq) // tk
            for j in range(n_wide):
                step(j * tk, tk, False)
            if n_wide * tk < c * tq:
                step(n_wide * tk, tq, False)
            step(c * tq, tq, True)

    for h in range(N_HEADS):
        cs = slice(h * HEAD_DIM, (h + 1) * HEAD_DIM)
        acc_scr[:, cs] = acc_scr[:, cs] / l_scr[h]
    out_ref[...] = _rms(acc_scr[...], g_ref[...]).astype(BF16)


def _fox_prompt(qb, kb, vb, ccol, crow, g_att, nseq, s, cast_jobs=(), tq=256, tk=512):
    assert tk == 2 * tq and s % tk == 0
    m = qb.shape[0]
    nq = s // tq
    stat = pltpu.VMEM((N_HEADS, tq, LANES), F32)
    cast_in, cast_out, cast_shapes = _cast_plumbing(cast_jobs, lambda b, i: b * nq + i)
    assert all(j.src.shape[1] // j.chunk_rows == nseq * nq for j in cast_jobs)
    return pl.pallas_call(
        functools.partial(_fox_kernel, tq=tq, tk=tk, nq=nq, n_casts=len(cast_jobs)),
        grid=(nseq, nq),
        in_specs=[pl.BlockSpec((tq, ATT_WIDTH), lambda b, i: (b * nq + i, 0)),
                  pl.BlockSpec((s, ATT_WIDTH), lambda b, i: (b, 0)),
                  pl.BlockSpec((s, ATT_WIDTH), lambda b, i: (b, 0)),
                  pl.BlockSpec((1, tq, N_HEADS), lambda b, i: (b, i, 0)),
                  pl.BlockSpec((1, N_HEADS, s), lambda b, i: (b, 0, 0)),
                  pl.BlockSpec((1, ATT_WIDTH), lambda b, i: (0, 0))] + cast_in,
        out_specs=[pl.BlockSpec((tq, ATT_WIDTH), lambda b, i: (b * nq + i, 0))] + cast_out,
        out_shape=[jax.ShapeDtypeStruct((m, ATT_WIDTH), BF16)] + cast_shapes,
        scratch_shapes=[stat, stat, stat, pltpu.VMEM((tq, ATT_WIDTH), F32)],
        compiler_params=_cparams(("arbitrary", "arbitrary")),
        name="fox_prompt",
    )(qb, kb, vb, ccol, crow, g_att, *[j.src for j in cast_jobs])


def _outproj_kernel(x_ref, ma_ref, mb_ref, wa_ref, wb_ref, o_ref):
    acc = jnp.dot(ma_ref[...], wa_ref[...], preferred_element_type=F32)
    acc = acc + jnp.dot(mb_ref[...], wb_ref[...], preferred_element_type=F32)
    o_ref[...] = x_ref[...] + acc


def _outproj(x, mix_a, mix_b, w_o_b, tm, tn=512):
    m = x.shape[0]
    half = D_MODEL // 2
    return pl.pallas_call(
        _outproj_kernel,
        grid=(m // tm, D_MODEL // tn),
        in_specs=[pl.BlockSpec((tm, tn), lambda i, n: (i, n)),
                  pl.BlockSpec((tm, half), lambda i, n: (i, 0)),
                  pl.BlockSpec((tm, half), lambda i, n: (i, 0)),
                  pl.BlockSpec((half, tn), lambda i, n: (0, n)),
                  pl.BlockSpec((half, tn), lambda i, n: (1, n))],
        out_specs=pl.BlockSpec((tm, tn), lambda i, n: (i, n)),
        out_shape=jax.ShapeDtypeStruct((m, D_MODEL), F32),
        compiler_params=_cparams(("arbitrary", "arbitrary")),
        name="outproj",
    )(x, mix_a, mix_b, w_o_b, w_o_b)


FFN_HALO = 8


def _ffn_kernel(*refs, tm, nf, tiles_per_seq, per_row_state):
    if per_row_state:
        (x_ref, g2_ref, wa_ref, wg_ref, cw_ref, cb_ref, wd_ref, prev_ref,
         o_ref, state_ref, xn_ref, act_scr, abuf, carry_ref) = refs
    else:
        (x_ref, g2_ref, wa_ref, wg_ref, cw_ref, cb_ref, wd_ref,
         o_ref, state_ref, xn_ref, act_scr, abuf, carry_ref) = refs
    mi = pl.program_id(0)
    f = pl.program_id(1)
    rc = min(tm, 256)

    @pl.when(f == 0)
    def _():
        def body(i, c):
            r = pl.ds(pl.multiple_of(i * rc, rc), rc)
            x = x_ref[r, :]
            xn_ref[r, :] = _rms(x, g2_ref[...]).astype(BF16)
            o_ref[r, :] = x
            return c
        lax.fori_loop(0, tm // rc, body, 0)

        @pl.when(mi == 0)
        def _():
            carry_ref[...] = jnp.zeros(carry_ref.shape, F32)

    def up():
        xn = xn_ref[...]
        a = jnp.dot(xn, wa_ref[...], preferred_element_type=F32)
        gate = jnp.dot(xn, wg_ref[...], preferred_element_type=F32)
        if per_row_state:
            p1 = prev_ref[1]
            ac = cw_ref[0:1, :] * prev_ref[0] + cw_ref[1:2, :] * p1 + cw_ref[2:3, :] * a
            state_ref[0] = p1
            state_ref[1] = a
        else:
            first = (mi % tiles_per_seq) == 0
            abuf[0:FFN_HALO, :] = jnp.where(first, 0.0, carry_ref[f])
            abuf[FFN_HALO:FFN_HALO + tm, :] = a
            carry_ref[f] = a[tm - FFN_HALO:tm, :]
            ac = (cw_ref[0:1, :] * abuf[FFN_HALO - 2:FFN_HALO - 2 + tm, :]
                  + cw_ref[1:2, :] * abuf[FFN_HALO - 1:FFN_HALO - 1 + tm, :]
                  + cw_ref[2:3, :] * a)
            state_ref[0] = a[tm - (FFN_K - 1):tm, :]
        ac = ac + cb_ref[...]
        act_scr[f % 2] = (ac * jax.nn.sigmoid(ac) * gate).astype(BF16)

    def down():
        o_ref[...] += jnp.dot(act_scr[(f + 1) % 2], wd_ref[...], preferred_element_type=F32)

    @pl.when(f == 0)
    def _():
        up()

    @pl.when((f > 0) & (f < nf))
    def _():
        down()
        up()

    @pl.when(f == nf)
    def _():
        down()


def _ffn(x, g2, w_up_b, cw, cb, w_down_b, prev, tm, seq_len, tf=512):
    m = x.shape[0]
    nf = D_FF // tf
    per_row_state = prev is not None
    cur = lambda f: jnp.minimum(f, nf - 1)
    old = lambda f: jnp.maximum(f - 1, 0)
    in_specs = [pl.BlockSpec((tm, D_MODEL), lambda i, f: (i, 0), pipeline_mode=pl.Buffered(1)),
                pl.BlockSpec((1, D_MODEL), lambda i, f: (0, 0)),
                pl.BlockSpec((D_MODEL, tf), lambda i, f: (0, cur(f))),
                pl.BlockSpec((D_MODEL, tf), lambda i, f: (0, nf + cur(f))),
                pl.BlockSpec((FFN_K, tf), lambda i, f: (0, cur(f))),
                pl.BlockSpec((1, tf), lambda i, f: (0, cur(f))),
                pl.BlockSpec((tf, D_MODEL), lambda i, f: (old(f), 0))]
    args = [x, g2, w_up_b, w_up_b, cw, cb, w_down_b]
    if per_row_state:
        tiles_per_seq = 1
        in_specs.append(pl.BlockSpec((FFN_K - 1, tm, tf), lambda i, f: (0, i, cur(f))))
        args.append(prev)
        state_spec = pl.BlockSpec((FFN_K - 1, tm, tf), lambda i, f: (0, i, cur(f)))
        state_shape = jax.ShapeDtypeStruct((FFN_K - 1, m, D_FF), F32)
    else:
        tiles_per_seq = seq_len // tm
        state_spec = pl.BlockSpec((1, FFN_K - 1, tf), lambda i, f: (i, 0, cur(f)))
        state_shape = jax.ShapeDtypeStruct((m // tm, FFN_K - 1, D_FF), F32)
    x_out, state = pl.pallas_call(
        functools.partial(_ffn_kernel, tm=tm, nf=nf, tiles_per_seq=tiles_per_seq, per_row_state=per_row_state),
        grid=(m // tm, nf + 1),
        in_specs=in_specs,
        out_specs=(pl.BlockSpec((tm, D_MODEL), lambda i, f: (i, 0)), state_spec),
        out_shape=(jax.ShapeDtypeStruct((m, D_MODEL), F32), state_shape),
        scratch_shapes=[pltpu.VMEM((tm, D_MODEL), BF16),
                        pltpu.VMEM((2, tm, tf), BF16),
                        pltpu.VMEM((FFN_HALO + tm, tf), F32),
                        pltpu.VMEM((nf, FFN_HALO, tf), F32)],
        compiler_params=_cparams(("arbitrary", "arbitrary")),
        name="ffn",
    )(*args)
    if not per_row_state:
        state = state[tiles_per_seq - 1::tiles_per_seq]
    return x_out, state


PAGE_ROW = PAGE_SIZE * N_HEADS


def _bias_kernel(pt_ref, pool_ref, lfnew_ref, bias_ref, lfbuf, *, db, n_pages):
    n_rows = db * n_pages

    def gather(r, c):
        lfbuf[pl.ds(r, 1), :] = pool_ref[pl.ds(pt_ref[r], 1), :]
        return c

    lax.fori_loop(0, n_rows, gather, 0)

    src = lax.broadcasted_iota(jnp.int32, (PAGE_ROW, PAGE_ROW), 0)
    dst = lax.broadcasted_iota(jnp.int32, (PAGE_ROW, PAGE_ROW), 1)
    same_head = (src & (N_HEADS - 1)) == (dst & (N_HEADS - 1))
    later = (src >> 3) > (dst >> 3)
    m_suffix = (same_head & later).astype(BF16)
    first = lax.broadcasted_iota(jnp.int32, (LANES, PAGE_ROW), 0)
    m_spread = (first == (lax.broadcasted_iota(jnp.int32, (LANES, PAGE_ROW), 1) & (N_HEADS - 1))).astype(BF16)
    pr = lax.broadcasted_iota(jnp.int32, (n_pages, n_pages), 0)
    pc = lax.broadcasted_iota(jnp.int32, (n_pages, n_pages), 1)
    later_page = (pc > pr).astype(BF16)

    lf = lfbuf[...]
    within = _dot01(_split3(lf), m_suffix)
    total = _dot01(_split3((within + lf)[:, :LANES]), m_spread)
    for b in range(db):
        rows = slice(b * n_pages, (b + 1) * n_pages)
        t0, t1, t2 = _split3(total[rows])
        after = jnp.dot(later_page, t0, preferred_element_type=F32)
        after = after + jnp.dot(later_page, t1, preferred_element_type=F32)
        after = after + jnp.dot(later_page, t2, preferred_element_type=F32)
        bias_ref[b] = within[rows] + after + lfnew_ref[b]


def _sample_bias(page_table, logf_flat, lfnew_rep):
    db, n_pages = page_table.shape
    grid_spec = pltpu.PrefetchScalarGridSpec(
        num_scalar_prefetch=1,
        grid=(1,),
        in_specs=[pl.BlockSpec(logf_flat.shape, lambda i, pt: (0, 0)),
                  pl.BlockSpec((db, 1, PAGE_ROW), lambda i, pt: (0, 0, 0))],
        out_specs=pl.BlockSpec((db, n_pages, PAGE_ROW), lambda i, pt: (0, 0, 0)),
        scratch_shapes=[pltpu.VMEM((db * n_pages, PAGE_ROW), F32)],
    )
    return pl.pallas_call(
        functools.partial(_bias_kernel, db=db, n_pages=n_pages),
        grid_spec=grid_spec,
        out_shape=jax.ShapeDtypeStruct((db, n_pages, PAGE_ROW), F32),
        compiler_params=_cparams(("arbitrary",)),
        name="sample_bias",
    )(page_table.reshape(db * n_pages), logf_flat, lfnew_rep)


def _head_allreduce(x, op, reduce_op):
    x = jnp.broadcast_to(reduce_op(x, axis=0, keepdims=True), (SUBLANES, LANES))
    for sh in (8, 16, 32, 64):
        x = op(x, pltpu.roll(x, sh, axis=1))
    return x


def _paged_kernel(pt_ref, q_ref, bias_ref, knew_ref, vnew_ref, *refs, n_steps):
    k_refs = refs[:PAGES_PER_STEP]
    v_refs = refs[PAGES_PER_STEP:2 * PAGES_PER_STEP]
    o_ref = refs[2 * PAGES_PER_STEP]
    m_scr, l_scr, acc_scr, s_scr = refs[2 * PAGES_PER_STEP + 1:]
    i = pl.program_id(1)

    @pl.when(i == 0)
    def _():
        m_scr[...] = jnp.full((SUBLANES, LANES), -jnp.inf, F32)
        l_scr[...] = jnp.zeros((SUBLANES, LANES), F32)
        acc_scr[...] = jnp.zeros((N_HEADS, HEAD_DIM), F32)

    q = q_ref[0]
    sub = lax.broadcasted_iota(jnp.int32, (N_HEADS, PAGE_ROW), 0)
    lane = lax.broadcasted_iota(jnp.int32, (N_HEADS, PAGE_ROW), 1)
    own_head = sub == (lane & (N_HEADS - 1))

    for j in range(PAGES_PER_STEP):
        k2 = k_refs[j][...].reshape(PAGE_ROW, HEAD_DIM).astype(BF16)
        st = lax.dot_general(q, k2, (((1,), (1,)), ((), ())), preferred_element_type=F32)
        s_scr[j:j + 1, :] = jnp.sum(jnp.where(own_head, st, 0.0), axis=0, keepdims=True)

    s = s_scr[...] + bias_ref[0] * LOG2E
    chunks = [s[:, c * LANES:(c + 1) * LANES] for c in range(PAGE_ROW // LANES)]
    mx = chunks[0]
    for c in chunks[1:]:
        mx = jnp.maximum(mx, c)
    m_prev = m_scr[...]
    m_new = jnp.maximum(m_prev, _head_allreduce(mx, jnp.maximum, jnp.max))
    alpha = jnp.exp2(m_prev - m_new)
    p = jnp.exp2(s - jnp.concatenate([m_new[0:1, :]] * (PAGE_ROW // LANES), axis=1))
    ps = p[:, 0:LANES]
    for c in range(1, PAGE_ROW // LANES):
        ps = ps + p[:, c * LANES:(c + 1) * LANES]
    l_scr[...] = alpha * l_scr[...] + _head_allreduce(ps, jnp.add, jnp.sum)
    m_scr[...] = m_new

    o = jnp.zeros((N_HEADS, HEAD_DIM), F32)
    for j in range(PAGES_PER_STEP):
        pj = jnp.where(own_head, jnp.broadcast_to(p[j:j + 1, :], (N_HEADS, PAGE_ROW)), 0.0).astype(BF16)
        v2 = v_refs[j][...].reshape(PAGE_ROW, HEAD_DIM).astype(BF16)
        o = o + jnp.dot(pj, v2, preferred_element_type=F32)

    sub8 = lax.broadcasted_iota(jnp.int32, (SUBLANES, LANES), 0)
    lane8 = lax.broadcasted_iota(jnp.int32, (SUBLANES, LANES), 1)
    diag = sub8 == lane8

    def to_col(x):
        return jnp.sum(jnp.where(diag, x, 0.0), axis=1, keepdims=True)

    acc = to_col(alpha) * acc_scr[...] + o
    acc_scr[...] = acc

    @pl.when(i == n_steps - 1)
    def _():
        m_col = to_col(m_new)
        l_col = to_col(l_scr[...])
        qf = q.astype(F32)
        kn = knew_ref[0].astype(BF16).astype(F32)
        s_new = jnp.sum(qf * kn, axis=1, keepdims=True)
        m_f = jnp.maximum(m_col, s_new)
        a_old = jnp.exp2(m_col - m_f)
        p_new = jnp.exp2(s_new - m_f)
        o_ref[0] = (acc * a_old + p_new * vnew_ref[0]) / (l_col * a_old + p_new)


def _paged_attention(layer, page_table, q3, bias, k_new3, v_new3, cache_k, cache_v):
    db, n_pages = page_table.shape
    n_steps = n_pages // PAGES_PER_STEP

    def page_spec(j):
        return pl.BlockSpec((None, None, PAGE_SIZE, N_HEADS, HEAD_DIM),
                            lambda b, i, pt: (layer, pt[b, i * PAGES_PER_STEP + j], 0, 0, 0))

    row = pl.BlockSpec((1, N_HEADS, HEAD_DIM), lambda b, i, pt: (b, 0, 0))
    grid_spec = pltpu.PrefetchScalarGridSpec(
        num_scalar_prefetch=1,
        grid=(db, n_steps),
        in_specs=[row,
                  pl.BlockSpec((1, PAGES_PER_STEP, PAGE_ROW), lambda b, i, pt: (b, i, 0)),
                  row, row]
                 + [page_spec(j) for j in range(PAGES_PER_STEP)]
                 + [page_spec(j) for j in range(PAGES_PER_STEP)],
        out_specs=row,
        scratch_shapes=[pltpu.VMEM((SUBLANES, LANES), F32), pltpu.VMEM((SUBLANES, LANES), F32),
                        pltpu.VMEM((N_HEADS, HEAD_DIM), F32), pltpu.VMEM((PAGES_PER_STEP, PAGE_ROW), F32)],
    )
    return pl.pallas_call(
        functools.partial(_paged_kernel, n_steps=n_steps),
        grid_spec=grid_spec,
        out_shape=jax.ShapeDtypeStruct((db, N_HEADS, HEAD_DIM), F32),
        compiler_params=_cparams(("arbitrary", "arbitrary")),
        name="paged_attention",
    )(page_table, q3, bias, k_new3, v_new3, *([cache_k] * PAGES_PER_STEP), *([cache_v] * PAGES_PER_STEP))


def _sample_mix_kernel(proj_ref, sp_ref, sc_ref, yatt_ref, pw_ref, ps_ref, cw_ref, gp_ref, gc_ref, ga_ref,
                       mixa_ref, mixb_ref, pstate_ref, cstate_ref):
    u = proj_ref[:, 0:POOL_WIDTH]
    hb = proj_ref[:, POOL_WIDTH:POOL_WIDTH + CONV_WIDTH]
    hc = proj_ref[:, POOL_WIDTH + CONV_WIDTH:POOL_WIDTH + 2 * CONV_WIDTH]
    hx = proj_ref[:, POOL_WIDTH + 2 * CONV_WIDTH:POOL_WIDTH + 3 * CONV_WIDTH]

    ys = []
    for g, win in enumerate(POOL_WINDOWS):
        cols = slice(g * POOL_GROUP_DIM, (g + 1) * POOL_GROUP_DIM)
        ug = u[:, cols]
        acc = ug
        for j in range(1, win):
            acc = acc + sp_ref[POOL_STATE - j, :, cols]
        cnt = float(min(POOL_STATE + 1, win))
        pooled = (acc / cnt - ug).astype(BF16)
        ys.append(jnp.dot(pooled, pw_ref[g], preferred_element_type=F32))
    y_pool = jnp.concatenate(ys, axis=1) * ps_ref[...]
    mixa_ref[:, 0:POOL_WIDTH] = _rms(y_pool, gp_ref[...]).astype(BF16)

    z = hc * hx
    conv = cw_ref[0:1, :] * sc_ref[0] + cw_ref[1:2, :] * sc_ref[1] + cw_ref[2:3, :] * z
    mixa_ref[:, POOL_WIDTH:POOL_WIDTH + CONV_WIDTH] = _rms(hb * conv, gc_ref[...]).astype(BF16)
    mixb_ref[...] = _rms(yatt_ref[...], ga_ref[...]).astype(BF16)

    for r in range(POOL_STATE - 1):
        pstate_ref[r] = sp_ref[r + 1]
    pstate_ref[POOL_STATE - 1] = u
    cstate_ref[0] = sc_ref[1]
    cstate_ref[1] = z


def _sample_mixers(proj, sp_t, sc_t, y_att, pool_w_b, pool_scale, conv_w, g_pool, g_conv, g_att):
    db = proj.shape[0]
    return pl.pallas_call(
        _sample_mix_kernel,
        out_shape=(jax.ShapeDtypeStruct((db, POOL_WIDTH + CONV_WIDTH), BF16),
                   jax.ShapeDtypeStruct((db, ATT_WIDTH), BF16),
                   jax.ShapeDtypeStruct((POOL_STATE, db, POOL_WIDTH), F32),
                   jax.ShapeDtypeStruct((CONV_K - 1, db, CONV_WIDTH), F32)),
        compiler_params=pltpu.CompilerParams(vmem_limit_bytes=VMEM_LIMIT),
        name="sample_mixers",
    )(proj, sp_t, sc_t, y_att, pool_w_b, pool_scale, conv_w, g_pool, g_conv, g_att)


def kernel(x_prompt, x_sample, cache_k, cache_v, cache_logf, state_pool, state_conv, state_ffn, page_table,
           norm1_g, w_in, b_f, pool_w, pool_scale, conv_w, q_norm_g, k_norm_g, out_norm_g, w_o, norm2_g,
           w_up, ffn_conv_w, ffn_conv_b, w_down):
    nb, s, _ = x_prompt.shape
    db = x_sample.shape[0]
    depth = w_in.shape[0]
    n_pool = cache_k.shape[1]

    xp = x_prompt.reshape(nb * s, D_MODEL)
    xs = x_sample.reshape(db, D_MODEL)
    logf_flat = cache_logf.reshape(depth, n_pool, PAGE_ROW)

    w_in_b_all = jnp.pad(w_in, ((0, 0), (0, 0), (0, D_PROJ + LANES - w_in.shape[2]))).astype(BF16)
    n_fox_steps = nb * (s // FOX_TQ)

    outs = {name: [] for name in ("fp", "pp", "cp", "ffp", "fs", "ps", "cs", "ffs")}
    kv_prompt = (jnp.zeros((depth, nb * s, ATT_WIDTH), F32), jnp.ones((depth, nb * s, ATT_WIDTH), F32))
    kv_sample = (jnp.zeros((depth, db, ATT_WIDTH), F32), jnp.ones((depth, db, ATT_WIDTH), F32))
    for l in range(depth):
        bf_pad = jnp.pad(b_f[l], (0, LANES - N_HEADS)).reshape(1, LANES)
        pool_w_b = pool_w[l].astype(BF16)
        g1 = norm1_g[l].reshape(1, D_MODEL)
        g2 = norm2_g[l].reshape(1, D_MODEL)
        qg = q_norm_g[l].reshape(1, HEAD_DIM)
        kg = k_norm_g[l].reshape(1, HEAD_DIM)
        ps_l = pool_scale[l].reshape(1, POOL_WIDTH)
        g_pool = out_norm_g[l, :POOL_WIDTH].reshape(1, POOL_WIDTH)
        g_conv = out_norm_g[l, POOL_WIDTH:POOL_WIDTH + CONV_WIDTH].reshape(1, CONV_WIDTH)
        g_att = out_norm_g[l, POOL_WIDTH + CONV_WIDTH:].reshape(1, ATT_WIDTH)
        cb = ffn_conv_b[l].reshape(1, D_FF)

        w_in_b = w_in_b_all
        proj, qb, kp_stack, kb, vp_stack, vb, logf, logf_pad = _inproj(
            l, xp, g1, w_in_b, bf_pad, qg, kg, tm=1024, kv_stack=kv_prompt)
        kv_prompt = (kp_stack, vp_stack)
        ccol, crow = _cumsum(logf_pad, nb, s)
        mix_a, pstate, cstate = _mixers(proj, pool_w_b, ps_l, conv_w[l], g_pool, g_conv, nb, s)
        own_weights = [_CastJob(w_o, l, D_MODEL // n_fox_steps, D_MODEL),
                       _CastJob(w_up, l, D_MODEL // n_fox_steps, 2 * D_FF),
                       _CastJob(w_down, l, D_FF // n_fox_steps, D_MODEL)]
        mix_b, w_o_b, w_up_b, w_down_b = _fox_prompt(qb, kb, vb, ccol, crow, g_att, nb, s, cast_jobs=own_weights,
                                                     tq=FOX_TQ, tk=2 * FOX_TQ)
        x_mid = _outproj(xp, mix_a, mix_b, w_o_b, tm=1024)
        xp, fstate = _ffn(x_mid, g2, w_up_b, ffn_conv_w[l], cb, w_down_b, None, tm=PROMPT_FFN_TM, seq_len=s)
        outs["fp"].append(logf.reshape(nb, s, N_HEADS))
        outs["pp"].append(pstate)
        outs["cp"].append(cstate)
        outs["ffp"].append(fstate)

        proj_s, qb_s, ks_stack, _, vs_stack, _, logf_s, _ = _inproj(
            l, xs, g1, w_in_b, bf_pad, qg, kg, tm=db, kv_stack=kv_sample)
        kv_sample = (ks_stack, vs_stack)
        k32_s, v32_s = ks_stack[l], vs_stack[l]
        lfnew_rep = jnp.tile(logf_s, (1, PAGE_SIZE)).reshape(db, 1, PAGE_ROW)
        bias = _sample_bias(page_table, logf_flat[l], lfnew_rep)
        y_att = _paged_attention(l, page_table, qb_s.reshape(db, N_HEADS, HEAD_DIM), bias,
                                 k32_s.reshape(db, N_HEADS, HEAD_DIM), v32_s.reshape(db, N_HEADS, HEAD_DIM),
                                 cache_k, cache_v)
        mix_a_s, mix_b_s, pstate_t, cstate_t = _sample_mixers(
            proj_s, jnp.swapaxes(state_pool[l], 0, 1), jnp.swapaxes(state_conv[l], 0, 1),
            y_att.reshape(db, ATT_WIDTH), pool_w_b, ps_l, conv_w[l], g_pool, g_conv, g_att)
        xs_mid = _outproj(xs, mix_a_s, mix_b_s, w_o_b, tm=db)
        xs, fstate_t = _ffn(xs_mid, g2, w_up_b, ffn_conv_w[l], cb, w_down_b,
                            jnp.swapaxes(state_ffn[l], 0, 1), tm=db, seq_len=1)
        outs["fs"].append(logf_s.reshape(db, 1, N_HEADS))
        outs["ps"].append(jnp.swapaxes(pstate_t, 0, 1))
        outs["cs"].append(jnp.swapaxes(cstate_t, 0, 1))
        outs["ffs"].append(jnp.swapaxes(fstate_t, 0, 1))

    st = lambda name: jnp.stack(outs[name])
    return (xp.reshape(nb, s, D_MODEL), xs.reshape(db, 1, D_MODEL),
            kv_prompt[0].reshape(depth, nb, s, N_HEADS, HEAD_DIM), kv_prompt[1].reshape(depth, nb, s, N_HEADS, HEAD_DIM),
            st("fp"), st("pp"), st("cp"), st("ffp"),
            kv_sample[0].reshape(depth, db, 1, N_HEADS, HEAD_DIM), kv_sample[1].reshape(depth, db, 1, N_HEADS, HEAD_DIM),
            st("fs"), st("ps"), st("cs"), st("ffs"))
```

```python
import functools
from typing import NamedTuple

import jax
import jax.numpy as jnp
from jax import lax
from jax.experimental import pallas as pl
from jax.experimental.pallas import tpu as pltpu

F32 = jnp.float32
BF16 = jnp.bfloat16

D_MODEL = 2048
PAGE_SIZE = 128
POOL_WIDTH = 512
POOL_WINDOWS = (2, 4, 8, 16)
POOL_GROUP_DIM = 128
POOL_STATE = 15
CONV_WIDTH = 512
CONV_K = 3
ATT_WIDTH = 1024
HEAD_DIM = 128
N_HEADS = 8
D_FF = 5632
FFN_K = 3
EPS = 1e-6
D_PROJ = POOL_WIDTH + 3 * CONV_WIDTH + 3 * ATT_WIDTH
SCALE = HEAD_DIM ** -0.5

LANES = 128
SUBLANES = 8
VMEM_LIMIT = 56 * 1024 * 1024

PROJ_TN = 512
PROMPT_FFN_TM = 1024
FOX_TQ = 256
LOG2E = 1.4426950408889634

T_Q = (POOL_WIDTH + 3 * CONV_WIDTH) // PROJ_TN
T_K = T_Q + ATT_WIDTH // PROJ_TN
T_V = T_K + ATT_WIDTH // PROJ_TN
T_END = D_PROJ // PROJ_TN
PAGES_PER_STEP = 16


def _cparams(sem):
    return pltpu.CompilerParams(dimension_semantics=sem, vmem_limit_bytes=VMEM_LIMIT)


def _rms(x, g):
    ms = jnp.mean(x * x, axis=-1, keepdims=True)
    return x * lax.rsqrt(ms + EPS) * g


def _log_sigmoid(z):
    return jnp.minimum(z, 0.0) - jnp.log1p(jnp.exp(-jnp.abs(z)))


def _split3(x):
    p0 = x.astype(BF16)
    r = x - p0.astype(F32)
    p1 = r.astype(BF16)
    p2 = (r - p1.astype(F32)).astype(BF16)
    return p0, p1, p2


def _dot01(a_pieces, b):
    out = jnp.dot(a_pieces[0], b, preferred_element_type=F32)
    for p in a_pieces[1:]:
        out = out + jnp.dot(p, b, preferred_element_type=F32)
    return out


class _CastJob(NamedTuple):
    src: jax.Array
    layer: int
    chunk_rows: int
    out_cols: int


def _cast_plumbing(jobs, step_of):
    in_specs, out_specs, out_shapes = [], [], []
    for job in jobs:
        _, rows, cols = job.src.shape
        n_chunks = rows // job.chunk_rows
        chunk = lambda *g, n_chunks=n_chunks: jnp.minimum(step_of(*g), n_chunks - 1)
        in_specs.append(pl.BlockSpec((None, job.chunk_rows, cols),
                                     lambda *g, job=job, chunk=chunk: (job.layer, chunk(*g), 0)))
        out_specs.append(pl.BlockSpec((job.chunk_rows, job.out_cols), lambda *g, chunk=chunk: (chunk(*g), 0)))
        out_shapes.append(jax.ShapeDtypeStruct((rows, job.out_cols), BF16))
    return in_specs, out_specs, out_shapes


def _run_casts(src_refs, dst_refs):
    for src, dst in zip(src_refs, dst_refs):
        cols = src.shape[1]
        if dst.shape[1] > cols:
            dst[...] = jnp.zeros(dst.shape, BF16)
        dst[:, :cols] = src[...].astype(BF16)


def _inproj_kernel(x_ref, g1_ref, w_ref, wf_ref, bf_ref, qg_ref, kg_ref, *rest, tm, n_aliased, n_casts, cast_steps):
    cast_src = rest[n_aliased:n_aliased + n_casts]
    proj_ref, q_ref, k32_ref, kb_ref, v32_ref, vb_ref, logf_ref, logfp_ref = rest[n_aliased + n_casts:][:8]
    cast_dst = rest[n_aliased + n_casts + 8:][:n_casts]
    xn_ref, y_scr = rest[-2:]
    n = pl.program_id(1)
    rc = min(tm, 256)

    if n_casts:
        @pl.when(pl.program_id(0) * pl.num_programs(1) + n < cast_steps)
        def _():
            _run_casts(cast_src, cast_dst)

    def normalise():
        def body(i, c):
            r = pl.ds(pl.multiple_of(i * rc, rc), rc)
            xn_ref[r, :] = _rms(x_ref[r, :], g1_ref[...]).astype(BF16)
            return c
        lax.fori_loop(0, tm // rc, body, 0)

    def matmul():
        return jnp.dot(xn_ref[...], w_ref[...], preferred_element_type=F32)

    def park():
        y_scr[n % 2] = matmul()

    def head_norm(g):
        y = y_scr[(n + 1) % 2]
        parts = []
        for h in range(PROJ_TN // HEAD_DIM):
            parts.append(_rms(y[:, h * HEAD_DIM:(h + 1) * HEAD_DIM], g))
        return jnp.concatenate(parts, axis=1)

    def finish_q():
        q_ref[...] = (head_norm(qg_ref[...]) * (SCALE * LOG2E)).astype(BF16)

    def finish_k():
        kn = head_norm(kg_ref[...])
        k32_ref[...] = kn
        kb_ref[...] = kn.astype(BF16)

    def direct_v():
        y = matmul()
        v32_ref[...] = y
        vb_ref[...] = y.astype(BF16)

    @pl.when(n == 0)
    def _():
        normalise()
        fl = jnp.dot(xn_ref[...], wf_ref[...], preferred_element_type=F32)
        lf = _log_sigmoid(fl + bf_ref[...])
        lane = lax.broadcasted_iota(jnp.int32, lf.shape, 1)
        lf = jnp.where(lane < N_HEADS, lf, 0.0)
        logfp_ref[...] = lf
        logf_ref[...] = lf[:, :N_HEADS]

    @pl.when(n < T_Q)
    def _():
        proj_ref[...] = matmul()

    @pl.when(n == T_Q)
    def _():
        park()

    @pl.when((n > T_Q) & (n <= T_K))
    def _():
        finish_q()
        park()

    @pl.when((n > T_K) & (n < T_V))
    def _():
        finish_k()
        park()

    @pl.when(n == T_V)
    def _():
        finish_k()
        direct_v()

    @pl.when(n > T_V)
    def _():
        direct_v()


def _inproj(layer, x, g1, w_in_b, bf_pad, qg, kg, tm, kv_stack, cast_jobs=()):
    m = x.shape[0]
    depth = kv_stack[0].shape[0]
    n_steps = T_END
    col = lambda first_step: (lambda i, n: (i, jnp.clip(n - first_step, 0, 1)))
    slab = lambda first_step: (lambda i, n: (layer, i, jnp.clip(n - first_step, 0, 1)))
    out_shape = (
        jax.ShapeDtypeStruct((m, 4 * PROJ_TN), F32),
        jax.ShapeDtypeStruct((m, ATT_WIDTH), BF16),
        jax.ShapeDtypeStruct((depth, m, ATT_WIDTH), F32),
        jax.ShapeDtypeStruct((m, ATT_WIDTH), BF16),
        jax.ShapeDtypeStruct((depth, m, ATT_WIDTH), F32),
        jax.ShapeDtypeStruct((m, ATT_WIDTH), BF16),
        jax.ShapeDtypeStruct((m, N_HEADS), F32),
        jax.ShapeDtypeStruct((m, LANES), F32),
    )
    out_specs = (
        pl.BlockSpec((tm, PROJ_TN), lambda i, n: (i, jnp.minimum(n, T_Q - 1))),
        pl.BlockSpec((tm, PROJ_TN), col(T_Q + 1)),
        pl.BlockSpec((None, tm, PROJ_TN), slab(T_K + 1)),
        pl.BlockSpec((tm, PROJ_TN), col(T_K + 1)),
        pl.BlockSpec((None, tm, PROJ_TN), slab(T_V)),
        pl.BlockSpec((tm, PROJ_TN), col(T_V)),
        pl.BlockSpec((tm, N_HEADS), lambda i, n: (i, 0)),
        pl.BlockSpec((tm, LANES), lambda i, n: (i, 0)),
    )
    in_specs = [
        pl.BlockSpec((tm, D_MODEL), lambda i, n: (i, 0)),
        pl.BlockSpec((1, D_MODEL), lambda i, n: (0, 0)),
        pl.BlockSpec((None, D_MODEL, PROJ_TN), lambda i, n: (layer, 0, n)),
        pl.BlockSpec((None, D_MODEL, LANES), lambda i, n: (layer, 0, D_PROJ // LANES)),
        pl.BlockSpec((1, LANES), lambda i, n: (0, 0)),
        pl.BlockSpec((1, HEAD_DIM), lambda i, n: (0, 0)),
        pl.BlockSpec((1, HEAD_DIM), lambda i, n: (0, 0)),
        pl.BlockSpec(memory_space=pl.ANY), pl.BlockSpec(memory_space=pl.ANY),
    ]
    args = [x, g1, w_in_b, w_in_b, bf_pad, qg, kg, *kv_stack]
    aliases = {7: 2, 8: 4}
    cast_in, cast_out, cast_shapes = _cast_plumbing(cast_jobs, lambda i, n: i * n_steps + n)
    cast_steps = max([j.src.shape[1] // j.chunk_rows for j in cast_jobs], default=0)
    assert cast_steps <= (m // tm) * n_steps
    return pl.pallas_call(
        functools.partial(_inproj_kernel, tm=tm, n_aliased=len(aliases), n_casts=len(cast_jobs),
                          cast_steps=cast_steps),
        grid=(m // tm, n_steps),
        in_specs=in_specs + cast_in, out_specs=out_specs + tuple(cast_out), out_shape=out_shape + tuple(cast_shapes),
        scratch_shapes=[pltpu.VMEM((tm, D_MODEL), BF16), pltpu.VMEM((2, tm, PROJ_TN), F32)],
        input_output_aliases=aliases,
        compiler_params=_cparams(("arbitrary", "arbitrary")),
        name="inproj",
    )(*args, *[j.src for j in cast_jobs])


CUM_BLK = 256


def _cumsum_kernel(lf_ref, ccol_ref, crow_ref, *, s):
    r = lax.broadcasted_iota(jnp.int32, (CUM_BLK, CUM_BLK), 0)
    c = lax.broadcasted_iota(jnp.int32, (CUM_BLK, CUM_BLK), 1)
    tri = (c <= r).astype(BF16)
    carry = jnp.zeros((1, LANES), F32)
    for b in range(s // CUM_BLK):
        rows = slice(b * CUM_BLK, (b + 1) * CUM_BLK)
        lf = lf_ref[0, rows, :]
        pieces = _split3(lf)
        cs = jnp.dot(tri, pieces[0], preferred_element_type=F32)
        cs = cs + jnp.dot(tri, pieces[1], preferred_element_type=F32)
        cs = cs + jnp.dot(tri, pieces[2], preferred_element_type=F32)
        cs = cs + carry
        carry = cs[CUM_BLK - 1:CUM_BLK, :]
        cs2 = cs * LOG2E
        ccol_ref[0, rows, :] = cs2[:, :N_HEADS]
        crow_ref[0, :, rows] = cs2.T[:N_HEADS, :]


def _cumsum(logf_pad, nseq, s):
    lf3 = logf_pad.reshape(nseq, s, LANES)
    return pl.pallas_call(
        functools.partial(_cumsum_kernel, s=s),
        grid=(nseq,),
        in_specs=[pl.BlockSpec((1, s, LANES), lambda b: (b, 0, 0))],
        out_specs=(pl.BlockSpec((1, s, N_HEADS), lambda b: (b, 0, 0)),
                   pl.BlockSpec((1, N_HEADS, s), lambda b: (b, 0, 0))),
        out_shape=(jax.ShapeDtypeStruct((nseq, s, N_HEADS), F32),
                   jax.ShapeDtypeStruct((nseq, N_HEADS, s), F32)),
        compiler_params=_cparams(("arbitrary",)),
        name="logf_cumsum",
    )(lf3)


POOL_HALO = 16
CONV_HALO = 8


def _mix_kernel(u_ref, hb_ref, hc_ref, hx_ref, pw_ref, ps_ref, cw_ref, gp_ref, gc_ref,
                mix_ref, pstate_ref, cstate_ref, ubuf, zbuf, *, ts):
    si = pl.program_id(1)

    @pl.when(si == 0)
    def _():
        ubuf[0:POOL_HALO, :] = jnp.zeros((POOL_HALO, POOL_WIDTH), F32)
        zbuf[0:CONV_HALO, :] = jnp.zeros((CONV_HALO, CONV_WIDTH), F32)

    @pl.when(si > 0)
    def _():
        ubuf[0:POOL_HALO, :] = ubuf[ts:ts + POOL_HALO, :]
        zbuf[0:CONV_HALO, :] = zbuf[ts:ts + CONV_HALO, :]

    u = u_ref[...]
    ubuf[POOL_HALO:POOL_HALO + ts, :] = u
    z = hc_ref[...] * hx_ref[...]
    zbuf[CONV_HALO:CONV_HALO + ts, :] = z

    pos = si * ts + lax.broadcasted_iota(jnp.int32, (ts, 1), 0)
    ys = []
    for g, win in enumerate(POOL_WINDOWS):
        cols = slice(g * POOL_GROUP_DIM, (g + 1) * POOL_GROUP_DIM)
        ug = u[:, cols]
        acc = ug
        for j in range(1, win):
            acc = acc + ubuf[POOL_HALO - j:POOL_HALO - j + ts, cols]
        cnt = jnp.minimum(pos + 1, win).astype(F32)
        pooled = (acc / cnt - ug).astype(BF16)
        ys.append(jnp.dot(pooled, pw_ref[g], preferred_element_type=F32))
    y_pool = jnp.concatenate(ys, axis=1) * ps_ref[...]
    mix_ref[:, 0:POOL_WIDTH] = _rms(y_pool, gp_ref[...]).astype(BF16)

    conv = (cw_ref[0:1, :] * zbuf[CONV_HALO - 2:CONV_HALO - 2 + ts, :]
            + cw_ref[1:2, :] * zbuf[CONV_HALO - 1:CONV_HALO - 1 + ts, :]
            + cw_ref[2:3, :] * z)
    y_conv = hb_ref[...] * conv
    mix_ref[:, POOL_WIDTH:POOL_WIDTH + CONV_WIDTH] = _rms(y_conv, gc_ref[...]).astype(BF16)

    pstate_ref[0] = ubuf[POOL_HALO + ts - POOL_STATE:POOL_HALO + ts, :]
    cstate_ref[0] = zbuf[CONV_HALO + ts - (CONV_K - 1):CONV_HALO + ts, :]


def _mixers(proj, pool_w_b, pool_scale, conv_w, g_pool, g_conv, nseq, s, ts=512):
    m = proj.shape[0]
    nst = s // ts
    pcol = lambda j: pl.BlockSpec((ts, PROJ_TN), lambda b, i: (b * nst + i, j))
    const = lambda shape: pl.BlockSpec(shape, lambda b, i: (0,) * len(shape))
    return pl.pallas_call(
        functools.partial(_mix_kernel, ts=ts),
        grid=(nseq, nst),
        in_specs=[pcol(0), pcol(1), pcol(2), pcol(3),
                  const((len(POOL_WINDOWS), POOL_GROUP_DIM, POOL_GROUP_DIM)),
                  const((1, POOL_WIDTH)), const((CONV_K, CONV_WIDTH)),
                  const((1, POOL_WIDTH)), const((1, CONV_WIDTH))],
        out_specs=(pl.BlockSpec((ts, POOL_WIDTH + CONV_WIDTH), lambda b, i: (b * nst + i, 0)),
                   pl.BlockSpec((1, POOL_STATE, POOL_WIDTH), lambda b, i: (b, 0, 0)),
                   pl.BlockSpec((1, CONV_K - 1, CONV_WIDTH), lambda b, i: (b, 0, 0))),
        out_shape=(jax.ShapeDtypeStruct((m, POOL_WIDTH + CONV_WIDTH), BF16),
                   jax.ShapeDtypeStruct((nseq, POOL_STATE, POOL_WIDTH), F32),
                   jax.ShapeDtypeStruct((nseq, CONV_K - 1, CONV_WIDTH), F32)),
        scratch_shapes=[pltpu.VMEM((POOL_HALO + ts, POOL_WIDTH), F32),
                        pltpu.VMEM((CONV_HALO + ts, CONV_WIDTH), F32)],
        compiler_params=_cparams(("arbitrary", "arbitrary")),
        name="prompt_mixers",
    )(proj, proj, proj, proj, pool_w_b, pool_scale, conv_w, g_pool, g_conv)


def _fox_kernel(q_ref, k_ref, v_ref, cq_ref, ck_ref, g_ref, *rest, tq, tk, n_casts):
    cast_src = rest[:n_casts]
    out_ref = rest[n_casts]
    cast_dst = rest[n_casts + 1:2 * n_casts + 1]
    m_scr, l_scr, cq_scr, acc_scr = rest[2 * n_casts + 1:]
    _run_casts(cast_src, cast_dst)
    i = pl.program_id(1)
    causal = (lax.broadcasted_iota(jnp.int32, (tq, tq), 1) <= lax.broadcasted_iota(jnp.int32, (tq, tq), 0))

    for h in range(N_HEADS):
        m_scr[h] = jnp.full((tq, LANES), -jnp.inf, F32)
        l_scr[h] = jnp.zeros((tq, LANES), F32)
        cq_scr[h] = jnp.broadcast_to(cq_ref[0, :, h:h + 1], (tq, LANES))
    acc_scr[...] = jnp.zeros(acc_scr.shape, F32)

    def step(k0, width, masked):
        ks = pl.ds(k0, width)
        lanes = lambda x: jnp.concatenate([x] * (width // LANES), axis=1)
        ones = jnp.ones((width, HEAD_DIM), BF16)
        for h in range(N_HEADS):
            cs = slice(h * HEAD_DIM, (h + 1) * HEAD_DIM)
            s = lax.dot_general(q_ref[:, cs], k_ref[ks, cs], (((1,), (1,)), ((), ())),
                                preferred_element_type=F32)
            s = s + lanes(cq_scr[h]) - ck_ref[0, h:h + 1, ks]
            if masked:
                s = jnp.where(causal, s, -jnp.inf)
            m_prev = m_scr[h]
            m_new = jnp.maximum(m_prev, jnp.broadcast_to(jnp.max(s, axis=1, keepdims=True), (tq, LANES)))
            alpha = jnp.exp2(m_prev - m_new)
            p = jnp.exp2(s - lanes(m_new)).astype(BF16)
            pv = jnp.dot(p, jnp.concatenate([v_ref[ks, cs], ones], axis=1), preferred_element_type=F32)
            l_scr[h] = alpha * l_scr[h] + pv[:, HEAD_DIM:]
            acc_scr[:, cs] = alpha * acc_scr[:, cs] + pv[:, :HEAD_DIM]
            m_scr[h] = m_new

    def body(j, c):
        step(pl.multiple_of(j * tk, tk), tk, False)
        return c

    n_wide = (i * tq) // tk
    lax.fori_loop(0, n_wide, body, 0)

    @pl.when(n_wide * tk < i * tq)
    def _():
        step(pl.multiple_of(n_wide * tk, tq), tq, False)

    step(pl.multiple_of(i * tq, tq), tq, True)

    for h in range(N_HEADS):
        cs = slice(h * HEAD_DIM, (h + 1) * HEAD_DIM)
        acc_scr[:, cs] = acc_scr[:, cs] / l_scr[h]
    out_ref[...] = _rms(acc_scr[...], g_ref[...]).astype(BF16)


def _fox_prompt(qb, kb, vb, ccol, crow, g_att, nseq, s, cast_jobs=(), tq=256, tk=512):
    assert tk == 2 * tq and s % tk == 0
    m = qb.shape[0]
    nq = s // tq
    stat = pltpu.VMEM((N_HEADS, tq, LANES), F32)
    cast_in, cast_out, cast_shapes = _cast_plumbing(cast_jobs, lambda b, i: b * nq + i)
    assert all(j.src.shape[1] // j.chunk_rows == nseq * nq for j in cast_jobs)
    return pl.pallas_call(
        functools.partial(_fox_kernel, tq=tq, tk=tk, n_casts=len(cast_jobs)),
        grid=(nseq, nq),
        in_specs=[pl.BlockSpec((tq, ATT_WIDTH), lambda b, i: (b * nq + i, 0)),
                  pl.BlockSpec((s, ATT_WIDTH), lambda b, i: (b, 0)),
                  pl.BlockSpec((s, ATT_WIDTH), lambda b, i: (b, 0)),
                  pl.BlockSpec((1, tq, N_HEADS), lambda b, i: (b, i, 0)),
                  pl.BlockSpec((1, N_HEADS, s), lambda b, i: (b, 0, 0)),
                  pl.BlockSpec((1, ATT_WIDTH), lambda b, i: (0, 0))] + cast_in,
        out_specs=[pl.BlockSpec((tq, ATT_WIDTH), lambda b, i: (b * nq + i, 0))] + cast_out,
        out_shape=[jax.ShapeDtypeStruct((m, ATT_WIDTH), BF16)] + cast_shapes,
        scratch_shapes=[stat, stat, stat, pltpu.VMEM((tq, ATT_WIDTH), F32)],
        compiler_params=_cparams(("arbitrary", "arbitrary")),
        name="fox_prompt",
    )(qb, kb, vb, ccol, crow, g_att, *[j.src for j in cast_jobs])


def _outproj_kernel(x_ref, ma_ref, mb_ref, wa_ref, wb_ref, o_ref):
    acc = jnp.dot(ma_ref[...], wa_ref[...], preferred_element_type=F32)
    acc = acc + jnp.dot(mb_ref[...], wb_ref[...], preferred_element_type=F32)
    o_ref[...] = x_ref[...] + acc


def _outproj(x, mix_a, mix_b, w_o_b, tm, tn=512):
    m = x.shape[0]
    half = D_MODEL // 2
    return pl.pallas_call(
        _outproj_kernel,
        grid=(m // tm, D_MODEL // tn),
        in_specs=[pl.BlockSpec((tm, tn), lambda i, n: (i, n)),
                  pl.BlockSpec((tm, half), lambda i, n: (i, 0)),
                  pl.BlockSpec((tm, half), lambda i, n: (i, 0)),
                  pl.BlockSpec((half, tn), lambda i, n: (0, n)),
                  pl.BlockSpec((half, tn), lambda i, n: (1, n))],
        out_specs=pl.BlockSpec((tm, tn), lambda i, n: (i, n)),
        out_shape=jax.ShapeDtypeStruct((m, D_MODEL), F32),
        compiler_params=_cparams(("arbitrary", "arbitrary")),
        name="outproj",
    )(x, mix_a, mix_b, w_o_b, w_o_b)


FFN_HALO = 8
RIDER_PAD = 16


def _ffn_kernel(x_ref, g2_ref, wa_ref, wg_ref, cw_ref, cb_ref, wd_ref, xs_ref, prev_ref,
                o_ref, state_ref, os_ref, sstate_ref, xn_ref, xns_ref, act_scr, abuf, carry_ref,
                *, tm, nf, tiles_per_seq, n_rider):
    mi = pl.program_id(0)
    f = pl.program_id(1)
    rc = min(tm, 256)
    pad = RIDER_PAD - n_rider

    @pl.when(f == 0)
    def _():
        def body(i, c):
            r = pl.ds(pl.multiple_of(i * rc, rc), rc)
            x = x_ref[r, :]
            xn_ref[r, :] = _rms(x, g2_ref[...]).astype(BF16)
            o_ref[r, :] = x
            return c
        lax.fori_loop(0, tm // rc, body, 0)

        @pl.when(mi == 0)
        def _():
            carry_ref[...] = jnp.zeros(carry_ref.shape, F32)
            xs = xs_ref[...]
            os_ref[...] = xs
            xns_ref[...] = jnp.concatenate([_rms(xs, g2_ref[...]), jnp.zeros((pad, xs.shape[1]), F32)],
                                           axis=0).astype(BF16)

    def up(rider):
        xn = xn_ref[...]
        if rider:
            xn = jnp.concatenate([xn, xns_ref[...]], axis=0)
        a_all = jnp.dot(xn, wa_ref[...], preferred_element_type=F32)
        gate_all = jnp.dot(xn, wg_ref[...], preferred_element_type=F32)
        a = a_all[:tm]
        first = (mi % tiles_per_seq) == 0
        abuf[0:FFN_HALO, :] = jnp.where(first, 0.0, carry_ref[f])
        abuf[FFN_HALO:FFN_HALO + tm, :] = a
        carry_ref[f] = a[tm - FFN_HALO:tm, :]
        ac = (cw_ref[0:1, :] * abuf[FFN_HALO - 2:FFN_HALO - 2 + tm, :]
              + cw_ref[1:2, :] * abuf[FFN_HALO - 1:FFN_HALO - 1 + tm, :]
              + cw_ref[2:3, :] * a)
        state_ref[0] = a[tm - (FFN_K - 1):tm, :]
        if rider:
            a_s = a_all[tm:tm + n_rider]
            p1 = prev_ref[1]
            ac_s = cw_ref[0:1, :] * prev_ref[0] + cw_ref[1:2, :] * p1 + cw_ref[2:3, :] * a_s
            sstate_ref[0] = p1
            sstate_ref[1] = a_s
            ac = jnp.concatenate([ac, ac_s, jnp.zeros((pad, ac.shape[1]), F32)], axis=0)
        ac = ac + cb_ref[...]
        act = (ac * jax.nn.sigmoid(ac) * gate_all).astype(BF16)
        if rider:
            act_scr[f % 2] = act
        else:
            act_scr[f % 2, 0:tm, :] = act

    def down(rider):
        if rider:
            part = jnp.dot(act_scr[(f + 1) % 2], wd_ref[...], preferred_element_type=F32)
            o_ref[...] += part[:tm]
            os_ref[...] += part[tm:tm + n_rider]
        else:
            o_ref[...] += jnp.dot(act_scr[(f + 1) % 2, 0:tm, :], wd_ref[...], preferred_element_type=F32)

    for rider in (True, False):
        mine = (mi == 0) if rider else (mi != 0)

        @pl.when(mine & (f == 0))
        def _(rider=rider):
            up(rider)

        @pl.when(mine & (f > 0) & (f < nf))
        def _(rider=rider):
            down(rider)
            up(rider)

        @pl.when(mine & (f == nf))
        def _(rider=rider):
            down(rider)


def _ffn(x, xs, g2, w_up_b, cw, cb, w_down_b, prev_s, tm, seq_len, tf=512):
    m = x.shape[0]
    n_rider = xs.shape[0]
    nf = D_FF // tf
    tiles_per_seq = seq_len // tm
    cur = lambda f: jnp.minimum(f, nf - 1)
    old = lambda f: jnp.maximum(f - 1, 0)
    rider_tile = lambda i, f: jnp.where(i == 0, cur(f), nf - 1)
    in_specs = [pl.BlockSpec((tm, D_MODEL), lambda i, f: (i, 0), pipeline_mode=pl.Buffered(1)),
                pl.BlockSpec((1, D_MODEL), lambda i, f: (0, 0)),
                pl.BlockSpec((D_MODEL, tf), lambda i, f: (0, cur(f))),
                pl.BlockSpec((D_MODEL, tf), lambda i, f: (0, nf + cur(f))),
                pl.BlockSpec((FFN_K, tf), lambda i, f: (0, cur(f))),
                pl.BlockSpec((1, tf), lambda i, f: (0, cur(f))),
                pl.BlockSpec((tf, D_MODEL), lambda i, f: (old(f), 0)),
                pl.BlockSpec((n_rider, D_MODEL), lambda i, f: (0, 0)),
                pl.BlockSpec((FFN_K - 1, n_rider, tf), lambda i, f: (0, 0, rider_tile(i, f)))]
    out_specs = (pl.BlockSpec((tm, D_MODEL), lambda i, f: (i, 0)),
                 pl.BlockSpec((1, FFN_K - 1, tf), lambda i, f: (i, 0, cur(f))),
                 pl.BlockSpec((n_rider, D_MODEL), lambda i, f: (0, 0)),
                 pl.BlockSpec((FFN_K - 1, n_rider, tf), lambda i, f: (0, 0, rider_tile(i, f))))
    out_shape = (jax.ShapeDtypeStruct((m, D_MODEL), F32),
                 jax.ShapeDtypeStruct((m // tm, FFN_K - 1, D_FF), F32),
                 jax.ShapeDtypeStruct((n_rider, D_MODEL), F32),
                 jax.ShapeDtypeStruct((FFN_K - 1, n_rider, D_FF), F32))
    x_out, state, xs_out, state_s = pl.pallas_call(
        functools.partial(_ffn_kernel, tm=tm, nf=nf, tiles_per_seq=tiles_per_seq, n_rider=n_rider),
        grid=(m // tm, nf + 1),
        in_specs=in_specs, out_specs=out_specs, out_shape=out_shape,
        scratch_shapes=[pltpu.VMEM((tm, D_MODEL), BF16),
                        pltpu.VMEM((RIDER_PAD, D_MODEL), BF16),
                        pltpu.VMEM((2, tm + RIDER_PAD, tf), BF16),
                        pltpu.VMEM((FFN_HALO + tm, tf), F32),
                        pltpu.VMEM((nf, FFN_HALO, tf), F32)],
        compiler_params=_cparams(("arbitrary", "arbitrary")),
        name="ffn",
    )(x, g2, w_up_b, w_up_b, cw, cb, w_down_b, xs, prev_s)
    return x_out, state[tiles_per_seq - 1::tiles_per_seq], xs_out, state_s


PAGE_ROW = PAGE_SIZE * N_HEADS


def _bias_kernel(pt_ref, pool_ref, lfnew_ref, bias_ref, lfbuf, *, db, n_pages):
    n_rows = db * n_pages

    def gather(r, c):
        lfbuf[pl.ds(r, 1), :] = pool_ref[pl.ds(pt_ref[r], 1), :]
        return c

    lax.fori_loop(0, n_rows, gather, 0)

    src = lax.broadcasted_iota(jnp.int32, (PAGE_ROW, PAGE_ROW), 0)
    dst = lax.broadcasted_iota(jnp.int32, (PAGE_ROW, PAGE_ROW), 1)
    same_head = (src & (N_HEADS - 1)) == (dst & (N_HEADS - 1))
    later = (src >> 3) > (dst >> 3)
    m_suffix = (same_head & later).astype(BF16)
    first = lax.broadcasted_iota(jnp.int32, (LANES, PAGE_ROW), 0)
    m_spread = (first == (lax.broadcasted_iota(jnp.int32, (LANES, PAGE_ROW), 1) & (N_HEADS - 1))).astype(BF16)
    pr = lax.broadcasted_iota(jnp.int32, (n_pages, n_pages), 0)
    pc = lax.broadcasted_iota(jnp.int32, (n_pages, n_pages), 1)
    later_page = (pc > pr).astype(BF16)

    lf = lfbuf[...]
    within = _dot01(_split3(lf), m_suffix)
    total = _dot01(_split3((within + lf)[:, :LANES]), m_spread)
    for b in range(db):
        rows = slice(b * n_pages, (b + 1) * n_pages)
        t0, t1, t2 = _split3(total[rows])
        after = jnp.dot(later_page, t0, preferred_element_type=F32)
        after = after + jnp.dot(later_page, t1, preferred_element_type=F32)
        after = after + jnp.dot(later_page, t2, preferred_element_type=F32)
        bias_ref[b] = within[rows] + after + lfnew_ref[b]


def _sample_bias(page_table, logf_flat, lfnew_rep):
    db, n_pages = page_table.shape
    grid_spec = pltpu.PrefetchScalarGridSpec(
        num_scalar_prefetch=1,
        grid=(1,),
        in_specs=[pl.BlockSpec(logf_flat.shape, lambda i, pt: (0, 0)),
                  pl.BlockSpec((db, 1, PAGE_ROW), lambda i, pt: (0, 0, 0))],
        out_specs=pl.BlockSpec((db, n_pages, PAGE_ROW), lambda i, pt: (0, 0, 0)),
        scratch_shapes=[pltpu.VMEM((db * n_pages, PAGE_ROW), F32)],
    )
    return pl.pallas_call(
        functools.partial(_bias_kernel, db=db, n_pages=n_pages),
        grid_spec=grid_spec,
        out_shape=jax.ShapeDtypeStruct((db, n_pages, PAGE_ROW), F32),
        compiler_params=_cparams(("arbitrary",)),
        name="sample_bias",
    )(page_table.reshape(db * n_pages), logf_flat, lfnew_rep)


def _head_allreduce(x, op, reduce_op):
    x = jnp.broadcast_to(reduce_op(x, axis=0, keepdims=True), (SUBLANES, LANES))
    for sh in (8, 16, 32, 64):
        x = op(x, pltpu.roll(x, sh, axis=1))
    return x


def _paged_kernel(pt_ref, q_ref, bias_ref, knew_ref, vnew_ref, *refs, n_steps):
    k_refs = refs[:PAGES_PER_STEP]
    v_refs = refs[PAGES_PER_STEP:2 * PAGES_PER_STEP]
    o_ref = refs[2 * PAGES_PER_STEP]
    m_scr, l_scr, acc_scr, s_scr = refs[2 * PAGES_PER_STEP + 1:]
    i = pl.program_id(1)

    @pl.when(i == 0)
    def _():
        m_scr[...] = jnp.full((SUBLANES, LANES), -jnp.inf, F32)
        l_scr[...] = jnp.zeros((SUBLANES, LANES), F32)
        acc_scr[...] = jnp.zeros((N_HEADS, HEAD_DIM), F32)

    q = q_ref[0]
    sub = lax.broadcasted_iota(jnp.int32, (N_HEADS, PAGE_ROW), 0)
    lane = lax.broadcasted_iota(jnp.int32, (N_HEADS, PAGE_ROW), 1)
    own_head = sub == (lane & (N_HEADS - 1))

    for j in range(PAGES_PER_STEP):
        k2 = k_refs[j][...].reshape(PAGE_ROW, HEAD_DIM).astype(BF16)
        st = lax.dot_general(q, k2, (((1,), (1,)), ((), ())), preferred_element_type=F32)
        s_scr[j:j + 1, :] = jnp.sum(jnp.where(own_head, st, 0.0), axis=0, keepdims=True)

    s = s_scr[...] + bias_ref[0] * LOG2E
    chunks = [s[:, c * LANES:(c + 1) * LANES] for c in range(PAGE_ROW // LANES)]
    mx = chunks[0]
    for c in chunks[1:]:
        mx = jnp.maximum(mx, c)
    m_prev = m_scr[...]
    m_new = jnp.maximum(m_prev, _head_allreduce(mx, jnp.maximum, jnp.max))
    alpha = jnp.exp2(m_prev - m_new)
    p = jnp.exp2(s - jnp.concatenate([m_new[0:1, :]] * (PAGE_ROW // LANES), axis=1))
    ps = p[:, 0:LANES]
    for c in range(1, PAGE_ROW // LANES):
        ps = ps + p[:, c * LANES:(c + 1) * LANES]
    l_scr[...] = alpha * l_scr[...] + _head_allreduce(ps, jnp.add, jnp.sum)
    m_scr[...] = m_new

    o = jnp.zeros((N_HEADS, HEAD_DIM), F32)
    for j in range(PAGES_PER_STEP):
        pj = jnp.where(own_head, jnp.broadcast_to(p[j:j + 1, :], (N_HEADS, PAGE_ROW)), 0.0).astype(BF16)
        v2 = v_refs[j][...].reshape(PAGE_ROW, HEAD_DIM).astype(BF16)
        o = o + jnp.dot(pj, v2, preferred_element_type=F32)

    sub8 = lax.broadcasted_iota(jnp.int32, (SUBLANES, LANES), 0)
    lane8 = lax.broadcasted_iota(jnp.int32, (SUBLANES, LANES), 1)
    diag = sub8 == lane8

    def to_col(x):
        return jnp.sum(jnp.where(diag, x, 0.0), axis=1, keepdims=True)

    acc = to_col(alpha) * acc_scr[...] + o
    acc_scr[...] = acc

    @pl.when(i == n_steps - 1)
    def _():
        m_col = to_col(m_new)
        l_col = to_col(l_scr[...])
        qf = q.astype(F32)
        kn = knew_ref[0].astype(BF16).astype(F32)
        s_new = jnp.sum(qf * kn, axis=1, keepdims=True)
        m_f = jnp.maximum(m_col, s_new)
        a_old = jnp.exp2(m_col - m_f)
        p_new = jnp.exp2(s_new - m_f)
        o_ref[0] = (acc * a_old + p_new * vnew_ref[0]) / (l_col * a_old + p_new)


def _paged_attention(layer, page_table, q3, bias, k_new3, v_new3, cache_k, cache_v):
    db, n_pages = page_table.shape
    n_steps = n_pages // PAGES_PER_STEP

    def page_spec(j):
        return pl.BlockSpec((None, None, PAGE_SIZE, N_HEADS, HEAD_DIM),
                            lambda b, i, pt: (layer, pt[b, i * PAGES_PER_STEP + j], 0, 0, 0))

    row = pl.BlockSpec((1, N_HEADS, HEAD_DIM), lambda b, i, pt: (b, 0, 0))
    grid_spec = pltpu.PrefetchScalarGridSpec(
        num_scalar_prefetch=1,
        grid=(db, n_steps),
        in_specs=[row,
                  pl.BlockSpec((1, PAGES_PER_STEP, PAGE_ROW), lambda b, i, pt: (b, i, 0)),
                  row, row]
                 + [page_spec(j) for j in range(PAGES_PER_STEP)]
                 + [page_spec(j) for j in range(PAGES_PER_STEP)],
        out_specs=row,
        scratch_shapes=[pltpu.VMEM((SUBLANES, LANES), F32), pltpu.VMEM((SUBLANES, LANES), F32),
                        pltpu.VMEM((N_HEADS, HEAD_DIM), F32), pltpu.VMEM((PAGES_PER_STEP, PAGE_ROW), F32)],
    )
    return pl.pallas_call(
        functools.partial(_paged_kernel, n_steps=n_steps),
        grid_spec=grid_spec,
        out_shape=jax.ShapeDtypeStruct((db, N_HEADS, HEAD_DIM), F32),
        compiler_params=_cparams(("arbitrary", "arbitrary")),
        name="paged_attention",
    )(page_table, q3, bias, k_new3, v_new3, *([cache_k] * PAGES_PER_STEP), *([cache_v] * PAGES_PER_STEP))


def _sample_mix_kernel(proj_ref, sp_ref, sc_ref, yatt_ref, pw_ref, ps_ref, cw_ref, gp_ref, gc_ref, ga_ref,
                       mixa_ref, mixb_ref, pstate_ref, cstate_ref):
    u = proj_ref[:, 0:POOL_WIDTH]
    hb = proj_ref[:, POOL_WIDTH:POOL_WIDTH + CONV_WIDTH]
    hc = proj_ref[:, POOL_WIDTH + CONV_WIDTH:POOL_WIDTH + 2 * CONV_WIDTH]
    hx = proj_ref[:, POOL_WIDTH + 2 * CONV_WIDTH:POOL_WIDTH + 3 * CONV_WIDTH]

    ys = []
    for g, win in enumerate(POOL_WINDOWS):
        cols = slice(g * POOL_GROUP_DIM, (g + 1) * POOL_GROUP_DIM)
        ug = u[:, cols]
        acc = ug
        for j in range(1, win):
            acc = acc + sp_ref[POOL_STATE - j, :, cols]
        cnt = float(min(POOL_STATE + 1, win))
        pooled = (acc / cnt - ug).astype(BF16)
        ys.append(jnp.dot(pooled, pw_ref[g], preferred_element_type=F32))
    y_pool = jnp.concatenate(ys, axis=1) * ps_ref[...]
    mixa_ref[:, 0:POOL_WIDTH] = _rms(y_pool, gp_ref[...]).astype(BF16)

    z = hc * hx
    conv = cw_ref[0:1, :] * sc_ref[0] + cw_ref[1:2, :] * sc_ref[1] + cw_ref[2:3, :] * z
    mixa_ref[:, POOL_WIDTH:POOL_WIDTH + CONV_WIDTH] = _rms(hb * conv, gc_ref[...]).astype(BF16)
    mixb_ref[...] = _rms(yatt_ref[...], ga_ref[...]).astype(BF16)

    for r in range(POOL_STATE - 1):
        pstate_ref[r] = sp_ref[r + 1]
    pstate_ref[POOL_STATE - 1] = u
    cstate_ref[0] = sc_ref[1]
    cstate_ref[1] = z


def _sample_mixers(proj, sp_t, sc_t, y_att, pool_w_b, pool_scale, conv_w, g_pool, g_conv, g_att):
    db = proj.shape[0]
    return pl.pallas_call(
        _sample_mix_kernel,
        out_shape=(jax.ShapeDtypeStruct((db, POOL_WIDTH + CONV_WIDTH), BF16),
                   jax.ShapeDtypeStruct((db, ATT_WIDTH), BF16),
                   jax.ShapeDtypeStruct((POOL_STATE, db, POOL_WIDTH), F32),
                   jax.ShapeDtypeStruct((CONV_K - 1, db, CONV_WIDTH), F32)),
        compiler_params=pltpu.CompilerParams(vmem_limit_bytes=VMEM_LIMIT),
        name="sample_mixers",
    )(proj, sp_t, sc_t, y_att, pool_w_b, pool_scale, conv_w, g_pool, g_conv, g_att)


def kernel(x_prompt, x_sample, cache_k, cache_v, cache_logf, state_pool, state_conv, state_ffn, page_table,
           norm1_g, w_in, b_f, pool_w, pool_scale, conv_w, q_norm_g, k_norm_g, out_norm_g, w_o, norm2_g,
           w_up, ffn_conv_w, ffn_conv_b, w_down):
    nb, s, _ = x_prompt.shape
    db = x_sample.shape[0]
    depth = w_in.shape[0]
    n_pool = cache_k.shape[1]

    xp = x_prompt.reshape(nb * s, D_MODEL)
    xs = x_sample.reshape(db, D_MODEL)
    logf_flat = cache_logf.reshape(depth, n_pool, PAGE_ROW)

    w_in_b_all = jnp.pad(w_in, ((0, 0), (0, 0), (0, D_PROJ + LANES - w_in.shape[2]))).astype(BF16)
    n_fox_steps = nb * (s // FOX_TQ)

    outs = {name: [] for name in ("fp", "pp", "cp", "ffp", "fs", "ps", "cs", "ffs")}
    kv_prompt = (jnp.zeros((depth, nb * s, ATT_WIDTH), F32), jnp.ones((depth, nb * s, ATT_WIDTH), F32))
    kv_sample = (jnp.zeros((depth, db, ATT_WIDTH), F32), jnp.ones((depth, db, ATT_WIDTH), F32))
    for l in range(depth):
        bf_pad = jnp.pad(b_f[l], (0, LANES - N_HEADS)).reshape(1, LANES)
        pool_w_b = pool_w[l].astype(BF16)
        g1 = norm1_g[l].reshape(1, D_MODEL)
        g2 = norm2_g[l].reshape(1, D_MODEL)
        qg = q_norm_g[l].reshape(1, HEAD_DIM)
        kg = k_norm_g[l].reshape(1, HEAD_DIM)
        ps_l = pool_scale[l].reshape(1, POOL_WIDTH)
        g_pool = out_norm_g[l, :POOL_WIDTH].reshape(1, POOL_WIDTH)
        g_conv = out_norm_g[l, POOL_WIDTH:POOL_WIDTH + CONV_WIDTH].reshape(1, CONV_WIDTH)
        g_att = out_norm_g[l, POOL_WIDTH + CONV_WIDTH:].reshape(1, ATT_WIDTH)
        cb = ffn_conv_b[l].reshape(1, D_FF)

        w_in_b = w_in_b_all
        proj, qb, kp_stack, kb, vp_stack, vb, logf, logf_pad = _inproj(
            l, xp, g1, w_in_b, bf_pad, qg, kg, tm=1024, kv_stack=kv_prompt)
        kv_prompt = (kp_stack, vp_stack)
        ccol, crow = _cumsum(logf_pad, nb, s)
        mix_a, pstate, cstate = _mixers(proj, pool_w_b, ps_l, conv_w[l], g_pool, g_conv, nb, s)
        own_weights = [_CastJob(w_o, l, D_MODEL // n_fox_steps, D_MODEL),
                       _CastJob(w_up, l, D_MODEL // n_fox_steps, 2 * D_FF),
                       _CastJob(w_down, l, D_FF // n_fox_steps, D_MODEL)]
        mix_b, w_o_b, w_up_b, w_down_b = _fox_prompt(qb, kb, vb, ccol, crow, g_att, nb, s, cast_jobs=own_weights,
                                                     tq=FOX_TQ, tk=2 * FOX_TQ)
        x_mid = _outproj(xp, mix_a, mix_b, w_o_b, tm=1024)
        outs["fp"].append(logf.reshape(nb, s, N_HEADS))
        outs["pp"].append(pstate)
        outs["cp"].append(cstate)

        proj_s, qb_s, ks_stack, _, vs_stack, _, logf_s, _ = _inproj(
            l, xs, g1, w_in_b, bf_pad, qg, kg, tm=db, kv_stack=kv_sample)
        kv_sample = (ks_stack, vs_stack)
        k32_s, v32_s = ks_stack[l], vs_stack[l]
        lfnew_rep = jnp.tile(logf_s, (1, PAGE_SIZE)).reshape(db, 1, PAGE_ROW)
        bias = _sample_bias(page_table, logf_flat[l], lfnew_rep)
        y_att = _paged_attention(l, page_table, qb_s.reshape(db, N_HEADS, HEAD_DIM), bias,
                                 k32_s.reshape(db, N_HEADS, HEAD_DIM), v32_s.reshape(db, N_HEADS, HEAD_DIM),
                                 cache_k, cache_v)
        mix_a_s, mix_b_s, pstate_t, cstate_t = _sample_mixers(
            proj_s, jnp.swapaxes(state_pool[l], 0, 1), jnp.swapaxes(state_conv[l], 0, 1),
            y_att.reshape(db, ATT_WIDTH), pool_w_b, ps_l, conv_w[l], g_pool, g_conv, g_att)
        xs_mid = _outproj(xs, mix_a_s, mix_b_s, w_o_b, tm=db)

        xp, fstate, xs, fstate_t = _ffn(x_mid, xs_mid, g2, w_up_b, ffn_conv_w[l], cb, w_down_b,
                                        jnp.swapaxes(state_ffn[l], 0, 1), tm=PROMPT_FFN_TM, seq_len=s)
        outs["ffp"].append(fstate)
        outs["fs"].append(logf_s.reshape(db, 1, N_HEADS))
        outs["ps"].append(jnp.swapaxes(pstate_t, 0, 1))
        outs["cs"].append(jnp.swapaxes(cstate_t, 0, 1))
        outs["ffs"].append(jnp.swapaxes(fstate_t, 0, 1))

    st = lambda name: jnp.stack(outs[name])
    return (xp.reshape(nb, s, D_MODEL), xs.reshape(db, 1, D_MODEL),
            kv_prompt[0].reshape(depth, nb, s, N_HEADS, HEAD_DIM), kv_prompt[1].reshape(depth, nb, s, N_HEADS, HEAD_DIM),
            st("fp"), st("pp"), st("cp"), st("ffp"),
            kv_sample[0].reshape(depth, db, 1, N_HEADS, HEAD_DIM), kv_sample[1].reshape(depth, db, 1, N_HEADS, HEAD_DIM),
            st("fs"), st("ps"), st("cs"), st("ffs"))
```

```python
import functools
from typing import NamedTuple

import jax
import jax.numpy as jnp
from jax import lax
from jax.experimental import pallas as pl
from jax.experimental.pallas import tpu as pltpu

F32 = jnp.float32
BF16 = jnp.bfloat16

D_MODEL = 2048
PAGE_SIZE = 128
POOL_WIDTH = 512
POOL_WINDOWS = (2, 4, 8, 16)
POOL_GROUP_DIM = 128
POOL_STATE = 15
CONV_WIDTH = 512
CONV_K = 3
ATT_WIDTH = 1024
HEAD_DIM = 128
N_HEADS = 8
D_FF = 5632
FFN_K = 3
EPS = 1e-6
D_PROJ = POOL_WIDTH + 3 * CONV_WIDTH + 3 * ATT_WIDTH
SCALE = HEAD_DIM ** -0.5

LANES = 128
SUBLANES = 8
VMEM_LIMIT = 56 * 1024 * 1024

PROJ_TN = 512
PROMPT_FFN_TM = 1024
FOX_TQ = 256
LOG2E = 1.4426950408889634

T_Q = (POOL_WIDTH + 3 * CONV_WIDTH) // PROJ_TN
T_K = T_Q + ATT_WIDTH // PROJ_TN
T_V = T_K + ATT_WIDTH // PROJ_TN
T_END = D_PROJ // PROJ_TN
PAGES_PER_STEP = 16


def _cparams(sem):
    return pltpu.CompilerParams(dimension_semantics=sem, vmem_limit_bytes=VMEM_LIMIT)


def _rms(x, g):
    ms = jnp.mean(x * x, axis=-1, keepdims=True)
    return x * lax.rsqrt(ms + EPS) * g


def _log_sigmoid(z):
    return jnp.minimum(z, 0.0) - jnp.log1p(jnp.exp(-jnp.abs(z)))


def _split3(x):
    p0 = x.astype(BF16)
    r = x - p0.astype(F32)
    p1 = r.astype(BF16)
    p2 = (r - p1.astype(F32)).astype(BF16)
    return p0, p1, p2


def _dot01(a_pieces, b):
    out = jnp.dot(a_pieces[0], b, preferred_element_type=F32)
    for p in a_pieces[1:]:
        out = out + jnp.dot(p, b, preferred_element_type=F32)
    return out


class _CastJob(NamedTuple):
    src: jax.Array
    layer: int
    chunk_rows: int
    out_cols: int


def _cast_plumbing(jobs, step_of):
    in_specs, out_specs, out_shapes = [], [], []
    for job in jobs:
        _, rows, cols = job.src.shape
        n_chunks = rows // job.chunk_rows
        chunk = lambda *g, n_chunks=n_chunks: jnp.minimum(step_of(*g), n_chunks - 1)
        in_specs.append(pl.BlockSpec((None, job.chunk_rows, cols),
                                     lambda *g, job=job, chunk=chunk: (job.layer, chunk(*g), 0)))
        out_specs.append(pl.BlockSpec((job.chunk_rows, job.out_cols), lambda *g, chunk=chunk: (chunk(*g), 0)))
        out_shapes.append(jax.ShapeDtypeStruct((rows, job.out_cols), BF16))
    return in_specs, out_specs, out_shapes


def _run_casts(src_refs, dst_refs):
    for src, dst in zip(src_refs, dst_refs):
        cols = src.shape[1]
        if dst.shape[1] > cols:
            dst[...] = jnp.zeros(dst.shape, BF16)
        dst[:, :cols] = src[...].astype(BF16)


def _inproj_kernel(x_ref, g1_ref, w_ref, wf_ref, bf_ref, qg_ref, kg_ref, *rest, tm, n_aliased, n_casts, cast_steps):
    cast_src = rest[n_aliased:n_aliased + n_casts]
    proj_ref, q_ref, k32_ref, kb_ref, v32_ref, vb_ref, logf_ref, logfp_ref = rest[n_aliased + n_casts:][:8]
    cast_dst = rest[n_aliased + n_casts + 8:][:n_casts]
    xn_ref, y_scr = rest[-2:]
    n = pl.program_id(1)
    rc = min(tm, 256)

    if n_casts:
        @pl.when(pl.program_id(0) * pl.num_programs(1) + n < cast_steps)
        def _():
            _run_casts(cast_src, cast_dst)

    def normalise():
        def body(i, c):
            r = pl.ds(pl.multiple_of(i * rc, rc), rc)
            xn_ref[r, :] = _rms(x_ref[r, :], g1_ref[...]).astype(BF16)
            return c
        lax.fori_loop(0, tm // rc, body, 0)

    def matmul():
        return jnp.dot(xn_ref[...], w_ref[...], preferred_element_type=F32)

    def park():
        y_scr[n % 2] = matmul()

    def head_norm(g):
        y = y_scr[(n + 1) % 2]
        parts = []
        for h in range(PROJ_TN // HEAD_DIM):
            parts.append(_rms(y[:, h * HEAD_DIM:(h + 1) * HEAD_DIM], g))
        return jnp.concatenate(parts, axis=1)

    def finish_q():
        q_ref[...] = (head_norm(qg_ref[...]) * (SCALE * LOG2E)).astype(BF16)

    def finish_k():
        kn = head_norm(kg_ref[...])
        k32_ref[...] = kn
        kb_ref[...] = kn.astype(BF16)

    def direct_v():
        y = matmul()
        v32_ref[...] = y
        vb_ref[...] = y.astype(BF16)

    @pl.when(n == 0)
    def _():
        normalise()
        fl = jnp.dot(xn_ref[...], wf_ref[...], preferred_element_type=F32)
        lf = _log_sigmoid(fl + bf_ref[...])
        lane = lax.broadcasted_iota(jnp.int32, lf.shape, 1)
        lf = jnp.where(lane < N_HEADS, lf, 0.0)
        logfp_ref[...] = lf
        logf_ref[...] = lf[:, :N_HEADS]

    @pl.when(n < T_Q)
    def _():
        proj_ref[...] = matmul()

    @pl.when(n == T_Q)
    def _():
        park()

    @pl.when((n > T_Q) & (n <= T_K))
    def _():
        finish_q()
        park()

    @pl.when((n > T_K) & (n < T_V))
    def _():
        finish_k()
        park()

    @pl.when(n == T_V)
    def _():
        finish_k()
        direct_v()

    @pl.when(n > T_V)
    def _():
        direct_v()


def _inproj(layer, x, g1, w_in_b, bf_pad, qg, kg, tm, kv_stack, cast_jobs=()):
    m = x.shape[0]
    depth = kv_stack[0].shape[0]
    n_steps = T_END
    col = lambda first_step: (lambda i, n: (i, jnp.clip(n - first_step, 0, 1)))
    slab = lambda first_step: (lambda i, n: (layer, i, jnp.clip(n - first_step, 0, 1)))
    out_shape = (
        jax.ShapeDtypeStruct((m, 4 * PROJ_TN), F32),
        jax.ShapeDtypeStruct((m, ATT_WIDTH), BF16),
        jax.ShapeDtypeStruct((depth, m, ATT_WIDTH), F32),
        jax.ShapeDtypeStruct((m, ATT_WIDTH), BF16),
        jax.ShapeDtypeStruct((depth, m, ATT_WIDTH), F32),
        jax.ShapeDtypeStruct((m, ATT_WIDTH), BF16),
        jax.ShapeDtypeStruct((m, N_HEADS), F32),
        jax.ShapeDtypeStruct((m, LANES), F32),
    )
    out_specs = (
        pl.BlockSpec((tm, PROJ_TN), lambda i, n: (i, jnp.minimum(n, T_Q - 1))),
        pl.BlockSpec((tm, PROJ_TN), col(T_Q + 1)),
        pl.BlockSpec((None, tm, PROJ_TN), slab(T_K + 1)),
        pl.BlockSpec((tm, PROJ_TN), col(T_K + 1)),
        pl.BlockSpec((None, tm, PROJ_TN), slab(T_V)),
        pl.BlockSpec((tm, PROJ_TN), col(T_V)),
        pl.BlockSpec((tm, N_HEADS), lambda i, n: (i, 0)),
        pl.BlockSpec((tm, LANES), lambda i, n: (i, 0)),
    )
    in_specs = [
        pl.BlockSpec((tm, D_MODEL), lambda i, n: (i, 0)),
        pl.BlockSpec((1, D_MODEL), lambda i, n: (0, 0)),
        pl.BlockSpec((None, D_MODEL, PROJ_TN), lambda i, n: (layer, 0, n)),
        pl.BlockSpec((None, D_MODEL, LANES), lambda i, n: (layer, 0, D_PROJ // LANES)),
        pl.BlockSpec((1, LANES), lambda i, n: (0, 0)),
        pl.BlockSpec((1, HEAD_DIM), lambda i, n: (0, 0)),
        pl.BlockSpec((1, HEAD_DIM), lambda i, n: (0, 0)),
        pl.BlockSpec(memory_space=pl.ANY), pl.BlockSpec(memory_space=pl.ANY),
    ]
    args = [x, g1, w_in_b, w_in_b, bf_pad, qg, kg, *kv_stack]
    aliases = {7: 2, 8: 4}
    cast_in, cast_out, cast_shapes = _cast_plumbing(cast_jobs, lambda i, n: i * n_steps + n)
    cast_steps = max([j.src.shape[1] // j.chunk_rows for j in cast_jobs], default=0)
    assert cast_steps <= (m // tm) * n_steps
    return pl.pallas_call(
        functools.partial(_inproj_kernel, tm=tm, n_aliased=len(aliases), n_casts=len(cast_jobs),
                          cast_steps=cast_steps),
        grid=(m // tm, n_steps),
        in_specs=in_specs + cast_in, out_specs=out_specs + tuple(cast_out), out_shape=out_shape + tuple(cast_shapes),
        scratch_shapes=[pltpu.VMEM((tm, D_MODEL), BF16), pltpu.VMEM((2, tm, PROJ_TN), F32)],
        input_output_aliases=aliases,
        compiler_params=_cparams(("arbitrary", "arbitrary")),
        name="inproj",
    )(*args, *[j.src for j in cast_jobs])


CUM_BLK = 256


def _cumsum_kernel(lf_ref, ccol_ref, crow_ref, *, s):
    r = lax.broadcasted_iota(jnp.int32, (CUM_BLK, CUM_BLK), 0)
    c = lax.broadcasted_iota(jnp.int32, (CUM_BLK, CUM_BLK), 1)
    tri = (c <= r).astype(BF16)
    carry = jnp.zeros((1, LANES), F32)
    for b in range(s // CUM_BLK):
        rows = slice(b * CUM_BLK, (b + 1) * CUM_BLK)
        lf = lf_ref[0, rows, :]
        pieces = _split3(lf)
        cs = jnp.dot(tri, pieces[0], preferred_element_type=F32)
        cs = cs + jnp.dot(tri, pieces[1], preferred_element_type=F32)
        cs = cs + jnp.dot(tri, pieces[2], preferred_element_type=F32)
        cs = cs + carry
        carry = cs[CUM_BLK - 1:CUM_BLK, :]
        cs2 = cs * LOG2E
        ccol_ref[0, rows, :] = cs2[:, :N_HEADS]
        crow_ref[0, :, rows] = cs2.T[:N_HEADS, :]


def _cumsum(logf_pad, nseq, s):
    lf3 = logf_pad.reshape(nseq, s, LANES)
    return pl.pallas_call(
        functools.partial(_cumsum_kernel, s=s),
        grid=(nseq,),
        in_specs=[pl.BlockSpec((1, s, LANES), lambda b: (b, 0, 0))],
        out_specs=(pl.BlockSpec((1, s, N_HEADS), lambda b: (b, 0, 0)),
                   pl.BlockSpec((1, N_HEADS, s), lambda b: (b, 0, 0))),
        out_shape=(jax.ShapeDtypeStruct((nseq, s, N_HEADS), F32),
                   jax.ShapeDtypeStruct((nseq, N_HEADS, s), F32)),
        compiler_params=_cparams(("arbitrary",)),
        name="logf_cumsum",
    )(lf3)


POOL_HALO = 16
CONV_HALO = 8


def _mix_kernel(u_ref, hb_ref, hc_ref, hx_ref, pw_ref, ps_ref, cw_ref, gp_ref, gc_ref,
                mix_ref, pstate_ref, cstate_ref, ubuf, zbuf, *, ts):
    si = pl.program_id(1)

    @pl.when(si == 0)
    def _():
        ubuf[0:POOL_HALO, :] = jnp.zeros((POOL_HALO, POOL_WIDTH), F32)
        zbuf[0:CONV_HALO, :] = jnp.zeros((CONV_HALO, CONV_WIDTH), F32)

    @pl.when(si > 0)
    def _():
        ubuf[0:POOL_HALO, :] = ubuf[ts:ts + POOL_HALO, :]
        zbuf[0:CONV_HALO, :] = zbuf[ts:ts + CONV_HALO, :]

    u = u_ref[...]
    ubuf[POOL_HALO:POOL_HALO + ts, :] = u
    z = hc_ref[...] * hx_ref[...]
    zbuf[CONV_HALO:CONV_HALO + ts, :] = z

    pos = si * ts + lax.broadcasted_iota(jnp.int32, (ts, 1), 0)
    ys = []
    for g, win in enumerate(POOL_WINDOWS):
        cols = slice(g * POOL_GROUP_DIM, (g + 1) * POOL_GROUP_DIM)
        ug = u[:, cols]
        acc = ug
        for j in range(1, win):
            acc = acc + ubuf[POOL_HALO - j:POOL_HALO - j + ts, cols]
        cnt = jnp.minimum(pos + 1, win).astype(F32)
        pooled = (acc / cnt - ug).astype(BF16)
        ys.append(jnp.dot(pooled, pw_ref[g], preferred_element_type=F32))
    y_pool = jnp.concatenate(ys, axis=1) * ps_ref[...]
    mix_ref[:, 0:POOL_WIDTH] = _rms(y_pool, gp_ref[...]).astype(BF16)

    conv = (cw_ref[0:1, :] * zbuf[CONV_HALO - 2:CONV_HALO - 2 + ts, :]
            + cw_ref[1:2, :] * zbuf[CONV_HALO - 1:CONV_HALO - 1 + ts, :]
            + cw_ref[2:3, :] * z)
    y_conv = hb_ref[...] * conv
    mix_ref[:, POOL_WIDTH:POOL_WIDTH + CONV_WIDTH] = _rms(y_conv, gc_ref[...]).astype(BF16)

    pstate_ref[0] = ubuf[POOL_HALO + ts - POOL_STATE:POOL_HALO + ts, :]
    cstate_ref[0] = zbuf[CONV_HALO + ts - (CONV_K - 1):CONV_HALO + ts, :]


def _mixers(proj, pool_w_b, pool_scale, conv_w, g_pool, g_conv, nseq, s, ts=512):
    m = proj.shape[0]
    nst = s // ts
    pcol = lambda j: pl.BlockSpec((ts, PROJ_TN), lambda b, i: (b * nst + i, j))
    const = lambda shape: pl.BlockSpec(shape, lambda b, i: (0,) * len(shape))
    return pl.pallas_call(
        functools.partial(_mix_kernel, ts=ts),
        grid=(nseq, nst),
        in_specs=[pcol(0), pcol(1), pcol(2), pcol(3),
                  const((len(POOL_WINDOWS), POOL_GROUP_DIM, POOL_GROUP_DIM)),
                  const((1, POOL_WIDTH)), const((CONV_K, CONV_WIDTH)),
                  const((1, POOL_WIDTH)), const((1, CONV_WIDTH))],
        out_specs=(pl.BlockSpec((ts, POOL_WIDTH + CONV_WIDTH), lambda b, i: (b * nst + i, 0)),
                   pl.BlockSpec((1, POOL_STATE, POOL_WIDTH), lambda b, i: (b, 0, 0)),
                   pl.BlockSpec((1, CONV_K - 1, CONV_WIDTH), lambda b, i: (b, 0, 0))),
        out_shape=(jax.ShapeDtypeStruct((m, POOL_WIDTH + CONV_WIDTH), BF16),
                   jax.ShapeDtypeStruct((nseq, POOL_STATE, POOL_WIDTH), F32),
                   jax.ShapeDtypeStruct((nseq, CONV_K - 1, CONV_WIDTH), F32)),
        scratch_shapes=[pltpu.VMEM((POOL_HALO + ts, POOL_WIDTH), F32),
                        pltpu.VMEM((CONV_HALO + ts, CONV_WIDTH), F32)],
        compiler_params=_cparams(("arbitrary", "arbitrary")),
        name="prompt_mixers",
    )(proj, proj, proj, proj, pool_w_b, pool_scale, conv_w, g_pool, g_conv)


def _fox_kernel(q_ref, k_ref, v_ref, cq_ref, ck_ref, g_ref, *rest, tq, tk, n_casts):
    cast_src = rest[:n_casts]
    out_ref = rest[n_casts]
    cast_dst = rest[n_casts + 1:2 * n_casts + 1]
    m_scr, l_scr, cq_scr, acc_scr = rest[2 * n_casts + 1:]
    _run_casts(cast_src, cast_dst)
    i = pl.program_id(1)
    causal = (lax.broadcasted_iota(jnp.int32, (tq, tq), 1) <= lax.broadcasted_iota(jnp.int32, (tq, tq), 0))

    for h in range(N_HEADS):
        m_scr[h] = jnp.full((tq, LANES), -jnp.inf, F32)
        l_scr[h] = jnp.zeros((tq, LANES), F32)
        cq_scr[h] = jnp.broadcast_to(cq_ref[0, :, h:h + 1], (tq, LANES))
    acc_scr[...] = jnp.zeros(acc_scr.shape, F32)

    def step(k0, width, masked):
        ks = pl.ds(k0, width)
        lanes = lambda x: jnp.concatenate([x] * (width // LANES), axis=1)
        ones = jnp.ones((width, HEAD_DIM), BF16)
        for h in range(N_HEADS):
            cs = slice(h * HEAD_DIM, (h + 1) * HEAD_DIM)
            s = lax.dot_general(q_ref[:, cs], k_ref[ks, cs], (((1,), (1,)), ((), ())),
                                preferred_element_type=F32)
            s = s + lanes(cq_scr[h]) - ck_ref[0, h:h + 1, ks]
            if masked:
                s = jnp.where(causal, s, -jnp.inf)
            m_prev = m_scr[h]
            m_new = jnp.maximum(m_prev, jnp.broadcast_to(jnp.max(s, axis=1, keepdims=True), (tq, LANES)))
            alpha = jnp.exp2(m_prev - m_new)
            p = jnp.exp2(s - lanes(m_new)).astype(BF16)
            pv = jnp.dot(p, jnp.concatenate([v_ref[ks, cs], ones], axis=1), preferred_element_type=F32)
            l_scr[h] = alpha * l_scr[h] + pv[:, HEAD_DIM:]
            acc_scr[:, cs] = alpha * acc_scr[:, cs] + pv[:, :HEAD_DIM]
            m_scr[h] = m_new

    def body(j, c):
        step(pl.multiple_of(j * tk, tk), tk, False)
        return c

    n_wide = (i * tq) // tk
    lax.fori_loop(0, n_wide, body, 0)

    @pl.when(n_wide * tk < i * tq)
    def _():
        step(pl.multiple_of(n_wide * tk, tq), tq, False)

    step(pl.multiple_of(i * tq, tq), tq, True)

    for h in range(N_HEADS):
        cs = slice(h * HEAD_DIM, (h + 1) * HEAD_DIM)
        acc_scr[:, cs] = acc_scr[:, cs] / l_scr[h]
    out_ref[...] = _rms(acc_scr[...], g_ref[...]).astype(BF16)


def _fox_prompt(qb, kb, vb, ccol, crow, g_att, nseq, s, cast_jobs=(), tq=256, tk=512):
    assert tk == 2 * tq and s % tk == 0
    m = qb.shape[0]
    nq = s // tq
    stat = pltpu.VMEM((N_HEADS, tq, LANES), F32)
    cast_in, cast_out, cast_shapes = _cast_plumbing(cast_jobs, lambda b, i: b * nq + i)
    assert all(j.src.shape[1] // j.chunk_rows == nseq * nq for j in cast_jobs)
    return pl.pallas_call(
        functools.partial(_fox_kernel, tq=tq, tk=tk, n_casts=len(cast_jobs)),
        grid=(nseq, nq),
        in_specs=[pl.BlockSpec((tq, ATT_WIDTH), lambda b, i: (b * nq + i, 0)),
                  pl.BlockSpec((s, ATT_WIDTH), lambda b, i: (b, 0)),
                  pl.BlockSpec((s, ATT_WIDTH), lambda b, i: (b, 0)),
                  pl.BlockSpec((1, tq, N_HEADS), lambda b, i: (b, i, 0)),
                  pl.BlockSpec((1, N_HEADS, s), lambda b, i: (b, 0, 0)),
                  pl.BlockSpec((1, ATT_WIDTH), lambda b, i: (0, 0))] + cast_in,
        out_specs=[pl.BlockSpec((tq, ATT_WIDTH), lambda b, i: (b * nq + i, 0))] + cast_out,
        out_shape=[jax.ShapeDtypeStruct((m, ATT_WIDTH), BF16)] + cast_shapes,
        scratch_shapes=[stat, stat, stat, pltpu.VMEM((tq, ATT_WIDTH), F32)],
        compiler_params=_cparams(("arbitrary", "arbitrary")),
        name="fox_prompt",
    )(qb, kb, vb, ccol, crow, g_att, *[j.src for j in cast_jobs])


def _outproj_kernel(x_ref, ma_ref, mb_ref, w_ref, xs_ref, mas_ref, mbs_ref, o_ref, os_ref, *, tm, n_rider):
    half = D_MODEL // 2

    def project(ma, mb):
        return (jnp.dot(ma, w_ref[0:half, :], preferred_element_type=F32)
                + jnp.dot(mb, w_ref[half:, :], preferred_element_type=F32))

    @pl.when(pl.program_id(0) == 0)
    def _():
        acc = project(jnp.concatenate([ma_ref[...], mas_ref[...]], axis=0),
                      jnp.concatenate([mb_ref[...], mbs_ref[...]], axis=0))
        o_ref[...] = x_ref[...] + acc[:tm]
        os_ref[...] = xs_ref[...] + acc[tm:tm + n_rider]

    @pl.when(pl.program_id(0) != 0)
    def _():
        o_ref[...] = x_ref[...] + project(ma_ref[...], mb_ref[...])


def _outproj(x, xs, mix_a, mix_b, mix_a_s, mix_b_s, w_o_b, tm=512):
    m = x.shape[0]
    n_rider = xs.shape[0]
    half = D_MODEL // 2
    pad_rows = lambda a: jnp.pad(a, ((0, RIDER_PAD - n_rider), (0, 0)))
    const = lambda shape, **kw: pl.BlockSpec(shape, lambda i: (0, 0), **kw)
    return pl.pallas_call(
        functools.partial(_outproj_kernel, tm=tm, n_rider=n_rider),
        grid=(m // tm,),
        in_specs=[pl.BlockSpec((tm, D_MODEL), lambda i: (i, 0)),
                  pl.BlockSpec((tm, half), lambda i: (i, 0)),
                  pl.BlockSpec((tm, half), lambda i: (i, 0)),
                  const((D_MODEL, D_MODEL), pipeline_mode=pl.Buffered(1)),
                  const((n_rider, D_MODEL)), const((RIDER_PAD, half)), const((RIDER_PAD, half))],
        out_specs=(pl.BlockSpec((tm, D_MODEL), lambda i: (i, 0)), const((n_rider, D_MODEL))),
        out_shape=(jax.ShapeDtypeStruct((m, D_MODEL), F32), jax.ShapeDtypeStruct((n_rider, D_MODEL), F32)),
        compiler_params=_cparams(("arbitrary",)),
        name="outproj",
    )(x, mix_a, mix_b, w_o_b, xs, pad_rows(mix_a_s), pad_rows(mix_b_s))


FFN_HALO = 8
RIDER_PAD = 16


def _ffn_kernel(x_ref, g2_ref, wa_ref, wg_ref, cw_ref, cb_ref, wd_ref, xs_ref, prev_ref,
                o_ref, state_ref, os_ref, sstate_ref, xn_ref, xns_ref, act_scr, abuf, carry_ref,
                *, tm, nf, tiles_per_seq, n_rider):
    mi = pl.program_id(0)
    f = pl.program_id(1)
    rc = min(tm, 256)
    pad = RIDER_PAD - n_rider

    @pl.when(f == 0)
    def _():
        def body(i, c):
            r = pl.ds(pl.multiple_of(i * rc, rc), rc)
            x = x_ref[r, :]
            xn_ref[r, :] = _rms(x, g2_ref[...]).astype(BF16)
            o_ref[r, :] = x
            return c
        lax.fori_loop(0, tm // rc, body, 0)

        @pl.when(mi == 0)
        def _():
            carry_ref[...] = jnp.zeros(carry_ref.shape, F32)
            xs = xs_ref[...]
            os_ref[...] = xs
            xns_ref[...] = jnp.concatenate([_rms(xs, g2_ref[...]), jnp.zeros((pad, xs.shape[1]), F32)],
                                           axis=0).astype(BF16)

    def up(rider):
        xn = xn_ref[...]
        if rider:
            xn = jnp.concatenate([xn, xns_ref[...]], axis=0)
        a_all = jnp.dot(xn, wa_ref[...], preferred_element_type=F32)
        gate_all = jnp.dot(xn, wg_ref[...], preferred_element_type=F32)
        a = a_all[:tm]
        first = (mi % tiles_per_seq) == 0
        abuf[0:FFN_HALO, :] = jnp.where(first, 0.0, carry_ref[f])
        abuf[FFN_HALO:FFN_HALO + tm, :] = a
        carry_ref[f] = a[tm - FFN_HALO:tm, :]
        ac = (cw_ref[0:1, :] * abuf[FFN_HALO - 2:FFN_HALO - 2 + tm, :]
              + cw_ref[1:2, :] * abuf[FFN_HALO - 1:FFN_HALO - 1 + tm, :]
              + cw_ref[2:3, :] * a)
        state_ref[0] = a[tm - (FFN_K - 1):tm, :]
        if rider:
            a_s = a_all[tm:tm + n_rider]
            p1 = prev_ref[1]
            ac_s = cw_ref[0:1, :] * prev_ref[0] + cw_ref[1:2, :] * p1 + cw_ref[2:3, :] * a_s
            sstate_ref[0] = p1
            sstate_ref[1] = a_s
            ac = jnp.concatenate([ac, ac_s, jnp.zeros((pad, ac.shape[1]), F32)], axis=0)
        ac = ac + cb_ref[...]
        act = (ac * jax.nn.sigmoid(ac) * gate_all).astype(BF16)
        if rider:
            act_scr[f % 2] = act
        else:
            act_scr[f % 2, 0:tm, :] = act

    def down(rider):
        if rider:
            part = jnp.dot(act_scr[(f + 1) % 2], wd_ref[...], preferred_element_type=F32)
            o_ref[...] += part[:tm]
            os_ref[...] += part[tm:tm + n_rider]
        else:
            o_ref[...] += jnp.dot(act_scr[(f + 1) % 2, 0:tm, :], wd_ref[...], preferred_element_type=F32)

    for rider in (True, False):
        mine = (mi == 0) if rider else (mi != 0)

        @pl.when(mine & (f == 0))
        def _(rider=rider):
            up(rider)

        @pl.when(mine & (f > 0) & (f < nf))
        def _(rider=rider):
            down(rider)
            up(rider)

        @pl.when(mine & (f == nf))
        def _(rider=rider):
            down(rider)


def _ffn(x, xs, g2, w_up_b, cw, cb, w_down_b, prev_s, tm, seq_len, tf=512):
    m = x.shape[0]
    n_rider = xs.shape[0]
    nf = D_FF // tf
    tiles_per_seq = seq_len // tm
    cur = lambda f: jnp.minimum(f, nf - 1)
    old = lambda f: jnp.maximum(f - 1, 0)
    rider_tile = lambda i, f: jnp.where(i == 0, cur(f), nf - 1)
    in_specs = [pl.BlockSpec((tm, D_MODEL), lambda i, f: (i, 0), pipeline_mode=pl.Buffered(1)),
                pl.BlockSpec((1, D_MODEL), lambda i, f: (0, 0)),
                pl.BlockSpec((D_MODEL, tf), lambda i, f: (0, cur(f))),
                pl.BlockSpec((D_MODEL, tf), lambda i, f: (0, nf + cur(f))),
                pl.BlockSpec((FFN_K, tf), lambda i, f: (0, cur(f))),
                pl.BlockSpec((1, tf), lambda i, f: (0, cur(f))),
                pl.BlockSpec((tf, D_MODEL), lambda i, f: (old(f), 0)),
                pl.BlockSpec((n_rider, D_MODEL), lambda i, f: (0, 0)),
                pl.BlockSpec((FFN_K - 1, n_rider, tf), lambda i, f: (0, 0, rider_tile(i, f)))]
    out_specs = (pl.BlockSpec((tm, D_MODEL), lambda i, f: (i, 0)),
                 pl.BlockSpec((1, FFN_K - 1, tf), lambda i, f: (i, 0, cur(f))),
                 pl.BlockSpec((n_rider, D_MODEL), lambda i, f: (0, 0)),
                 pl.BlockSpec((FFN_K - 1, n_rider, tf), lambda i, f: (0, 0, rider_tile(i, f))))
    out_shape = (jax.ShapeDtypeStruct((m, D_MODEL), F32),
                 jax.ShapeDtypeStruct((m // tm, FFN_K - 1, D_FF), F32),
                 jax.ShapeDtypeStruct((n_rider, D_MODEL), F32),
                 jax.ShapeDtypeStruct((FFN_K - 1, n_rider, D_FF), F32))
    x_out, state, xs_out, state_s = pl.pallas_call(
        functools.partial(_ffn_kernel, tm=tm, nf=nf, tiles_per_seq=tiles_per_seq, n_rider=n_rider),
        grid=(m // tm, nf + 1),
        in_specs=in_specs, out_specs=out_specs, out_shape=out_shape,
        scratch_shapes=[pltpu.VMEM((tm, D_MODEL), BF16),
                        pltpu.VMEM((RIDER_PAD, D_MODEL), BF16),
                        pltpu.VMEM((2, tm + RIDER_PAD, tf), BF16),
                        pltpu.VMEM((FFN_HALO + tm, tf), F32),
                        pltpu.VMEM((nf, FFN_HALO, tf), F32)],
        compiler_params=_cparams(("arbitrary", "arbitrary")),
        name="ffn",
    )(x, g2, w_up_b, w_up_b, cw, cb, w_down_b, xs, prev_s)
    return x_out, state[tiles_per_seq - 1::tiles_per_seq], xs_out, state_s


PAGE_ROW = PAGE_SIZE * N_HEADS


def _bias_kernel(pt_ref, pool_ref, lfnew_ref, bias_ref, lfbuf, *, db, n_pages):
    n_rows = db * n_pages

    def gather(r, c):
        lfbuf[pl.ds(r, 1), :] = pool_ref[pl.ds(pt_ref[r], 1), :]
        return c

    lax.fori_loop(0, n_rows, gather, 0)

    src = lax.broadcasted_iota(jnp.int32, (PAGE_ROW, PAGE_ROW), 0)
    dst = lax.broadcasted_iota(jnp.int32, (PAGE_ROW, PAGE_ROW), 1)
    same_head = (src & (N_HEADS - 1)) == (dst & (N_HEADS - 1))
    later = (src >> 3) > (dst >> 3)
    m_suffix = (same_head & later).astype(BF16)
    first = lax.broadcasted_iota(jnp.int32, (LANES, PAGE_ROW), 0)
    m_spread = (first == (lax.broadcasted_iota(jnp.int32, (LANES, PAGE_ROW), 1) & (N_HEADS - 1))).astype(BF16)
    pr = lax.broadcasted_iota(jnp.int32, (n_pages, n_pages), 0)
    pc = lax.broadcasted_iota(jnp.int32, (n_pages, n_pages), 1)
    later_page = (pc > pr).astype(BF16)

    lf = lfbuf[...]
    within = _dot01(_split3(lf), m_suffix)
    total = _dot01(_split3((within + lf)[:, :LANES]), m_spread)
    for b in range(db):
        rows = slice(b * n_pages, (b + 1) * n_pages)
        t0, t1, t2 = _split3(total[rows])
        after = jnp.dot(later_page, t0, preferred_element_type=F32)
        after = after + jnp.dot(later_page, t1, preferred_element_type=F32)
        after = after + jnp.dot(later_page, t2, preferred_element_type=F32)
        bias_ref[b] = within[rows] + after + lfnew_ref[b]


def _sample_bias(page_table, logf_flat, lfnew_rep):
    db, n_pages = page_table.shape
    grid_spec = pltpu.PrefetchScalarGridSpec(
        num_scalar_prefetch=1,
        grid=(1,),
        in_specs=[pl.BlockSpec(logf_flat.shape, lambda i, pt: (0, 0)),
                  pl.BlockSpec((db, 1, PAGE_ROW), lambda i, pt: (0, 0, 0))],
        out_specs=pl.BlockSpec((db, n_pages, PAGE_ROW), lambda i, pt: (0, 0, 0)),
        scratch_shapes=[pltpu.VMEM((db * n_pages, PAGE_ROW), F32)],
    )
    return pl.pallas_call(
        functools.partial(_bias_kernel, db=db, n_pages=n_pages),
        grid_spec=grid_spec,
        out_shape=jax.ShapeDtypeStruct((db, n_pages, PAGE_ROW), F32),
        compiler_params=_cparams(("arbitrary",)),
        name="sample_bias",
    )(page_table.reshape(db * n_pages), logf_flat, lfnew_rep)


def _head_allreduce(x, op, reduce_op):
    x = jnp.broadcast_to(reduce_op(x, axis=0, keepdims=True), (SUBLANES, LANES))
    for sh in (8, 16, 32, 64):
        x = op(x, pltpu.roll(x, sh, axis=1))
    return x


def _paged_kernel(pt_ref, q_ref, bias_ref, knew_ref, vnew_ref, *refs, n_steps):
    k_refs = refs[:PAGES_PER_STEP]
    v_refs = refs[PAGES_PER_STEP:2 * PAGES_PER_STEP]
    o_ref = refs[2 * PAGES_PER_STEP]
    m_scr, l_scr, acc_scr, s_scr = refs[2 * PAGES_PER_STEP + 1:]
    i = pl.program_id(1)

    @pl.when(i == 0)
    def _():
        m_scr[...] = jnp.full((SUBLANES, LANES), -jnp.inf, F32)
        l_scr[...] = jnp.zeros((SUBLANES, LANES), F32)
        acc_scr[...] = jnp.zeros((N_HEADS, HEAD_DIM), F32)

    q = q_ref[0]
    sub = lax.broadcasted_iota(jnp.int32, (N_HEADS, PAGE_ROW), 0)
    lane = lax.broadcasted_iota(jnp.int32, (N_HEADS, PAGE_ROW), 1)
    own_head = sub == (lane & (N_HEADS - 1))

    for j in range(PAGES_PER_STEP):
        k2 = k_refs[j][...].reshape(PAGE_ROW, HEAD_DIM).astype(BF16)
        st = lax.dot_general(q, k2, (((1,), (1,)), ((), ())), preferred_element_type=F32)
        s_scr[j:j + 1, :] = jnp.sum(jnp.where(own_head, st, 0.0), axis=0, keepdims=True)

    s = s_scr[...] + bias_ref[0] * LOG2E
    chunks = [s[:, c * LANES:(c + 1) * LANES] for c in range(PAGE_ROW // LANES)]
    mx = chunks[0]
    for c in chunks[1:]:
        mx = jnp.maximum(mx, c)
    m_prev = m_scr[...]
    m_new = jnp.maximum(m_prev, _head_allreduce(mx, jnp.maximum, jnp.max))
    alpha = jnp.exp2(m_prev - m_new)
    p = jnp.exp2(s - jnp.concatenate([m_new[0:1, :]] * (PAGE_ROW // LANES), axis=1))
    ps = p[:, 0:LANES]
    for c in range(1, PAGE_ROW // LANES):
        ps = ps + p[:, c * LANES:(c + 1) * LANES]
    l_scr[...] = alpha * l_scr[...] + _head_allreduce(ps, jnp.add, jnp.sum)
    m_scr[...] = m_new

    o = jnp.zeros((N_HEADS, HEAD_DIM), F32)
    for j in range(PAGES_PER_STEP):
        pj = jnp.where(own_head, jnp.broadcast_to(p[j:j + 1, :], (N_HEADS, PAGE_ROW)), 0.0).astype(BF16)
        v2 = v_refs[j][...].reshape(PAGE_ROW, HEAD_DIM).astype(BF16)
        o = o + jnp.dot(pj, v2, preferred_element_type=F32)

    sub8 = lax.broadcasted_iota(jnp.int32, (SUBLANES, LANES), 0)
    lane8 = lax.broadcasted_iota(jnp.int32, (SUBLANES, LANES), 1)
    diag = sub8 == lane8

    def to_col(x):
        return jnp.sum(jnp.where(diag, x, 0.0), axis=1, keepdims=True)

    acc = to_col(alpha) * acc_scr[...] + o
    acc_scr[...] = acc

    @pl.when(i == n_steps - 1)
    def _():
        m_col = to_col(m_new)
        l_col = to_col(l_scr[...])
        qf = q.astype(F32)
        kn = knew_ref[0].astype(BF16).astype(F32)
        s_new = jnp.sum(qf * kn, axis=1, keepdims=True)
        m_f = jnp.maximum(m_col, s_new)
        a_old = jnp.exp2(m_col - m_f)
        p_new = jnp.exp2(s_new - m_f)
        o_ref[0] = (acc * a_old + p_new * vnew_ref[0]) / (l_col * a_old + p_new)


def _paged_attention(layer, page_table, q3, bias, k_new3, v_new3, cache_k, cache_v):
    db, n_pages = page_table.shape
    n_steps = n_pages // PAGES_PER_STEP

    def page_spec(j):
        return pl.BlockSpec((None, None, PAGE_SIZE, N_HEADS, HEAD_DIM),
                            lambda b, i, pt: (layer, pt[b, i * PAGES_PER_STEP + j], 0, 0, 0))

    row = pl.BlockSpec((1, N_HEADS, HEAD_DIM), lambda b, i, pt: (b, 0, 0))
    grid_spec = pltpu.PrefetchScalarGridSpec(
        num_scalar_prefetch=1,
        grid=(db, n_steps),
        in_specs=[row,
                  pl.BlockSpec((1, PAGES_PER_STEP, PAGE_ROW), lambda b, i, pt: (b, i, 0)),
                  row, row]
                 + [page_spec(j) for j in range(PAGES_PER_STEP)]
                 + [page_spec(j) for j in range(PAGES_PER_STEP)],
        out_specs=row,
        scratch_shapes=[pltpu.VMEM((SUBLANES, LANES), F32), pltpu.VMEM((SUBLANES, LANES), F32),
                        pltpu.VMEM((N_HEADS, HEAD_DIM), F32), pltpu.VMEM((PAGES_PER_STEP, PAGE_ROW), F32)],
    )
    return pl.pallas_call(
        functools.partial(_paged_kernel, n_steps=n_steps),
        grid_spec=grid_spec,
        out_shape=jax.ShapeDtypeStruct((db, N_HEADS, HEAD_DIM), F32),
        compiler_params=_cparams(("arbitrary", "arbitrary")),
        name="paged_attention",
    )(page_table, q3, bias, k_new3, v_new3, *([cache_k] * PAGES_PER_STEP), *([cache_v] * PAGES_PER_STEP))


def _sample_mix_kernel(proj_ref, sp_ref, sc_ref, yatt_ref, pw_ref, ps_ref, cw_ref, gp_ref, gc_ref, ga_ref,
                       mixa_ref, mixb_ref, pstate_ref, cstate_ref):
    u = proj_ref[:, 0:POOL_WIDTH]
    hb = proj_ref[:, POOL_WIDTH:POOL_WIDTH + CONV_WIDTH]
    hc = proj_ref[:, POOL_WIDTH + CONV_WIDTH:POOL_WIDTH + 2 * CONV_WIDTH]
    hx = proj_ref[:, POOL_WIDTH + 2 * CONV_WIDTH:POOL_WIDTH + 3 * CONV_WIDTH]

    ys = []
    for g, win in enumerate(POOL_WINDOWS):
        cols = slice(g * POOL_GROUP_DIM, (g + 1) * POOL_GROUP_DIM)
        ug = u[:, cols]
        acc = ug
        for j in range(1, win):
            acc = acc + sp_ref[POOL_STATE - j, :, cols]
        cnt = float(min(POOL_STATE + 1, win))
        pooled = (acc / cnt - ug).astype(BF16)
        ys.append(jnp.dot(pooled, pw_ref[g], preferred_element_type=F32))
    y_pool = jnp.concatenate(ys, axis=1) * ps_ref[...]
    mixa_ref[:, 0:POOL_WIDTH] = _rms(y_pool, gp_ref[...]).astype(BF16)

    z = hc * hx
    conv = cw_ref[0:1, :] * sc_ref[0] + cw_ref[1:2, :] * sc_ref[1] + cw_ref[2:3, :] * z
    mixa_ref[:, POOL_WIDTH:POOL_WIDTH + CONV_WIDTH] = _rms(hb * conv, gc_ref[...]).astype(BF16)
    mixb_ref[...] = _rms(yatt_ref[...], ga_ref[...]).astype(BF16)

    for r in range(POOL_STATE - 1):
        pstate_ref[r] = sp_ref[r + 1]
    pstate_ref[POOL_STATE - 1] = u
    cstate_ref[0] = sc_ref[1]
    cstate_ref[1] = z


def _sample_mixers(proj, sp_t, sc_t, y_att, pool_w_b, pool_scale, conv_w, g_pool, g_conv, g_att):
    db = proj.shape[0]
    return pl.pallas_call(
        _sample_mix_kernel,
        out_shape=(jax.ShapeDtypeStruct((db, POOL_WIDTH + CONV_WIDTH), BF16),
                   jax.ShapeDtypeStruct((db, ATT_WIDTH), BF16),
                   jax.ShapeDtypeStruct((POOL_STATE, db, POOL_WIDTH), F32),
                   jax.ShapeDtypeStruct((CONV_K - 1, db, CONV_WIDTH), F32)),
        compiler_params=pltpu.CompilerParams(vmem_limit_bytes=VMEM_LIMIT),
        name="sample_mixers",
    )(proj, sp_t, sc_t, y_att, pool_w_b, pool_scale, conv_w, g_pool, g_conv, g_att)


def kernel(x_prompt, x_sample, cache_k, cache_v, cache_logf, state_pool, state_conv, state_ffn, page_table,
           norm1_g, w_in, b_f, pool_w, pool_scale, conv_w, q_norm_g, k_norm_g, out_norm_g, w_o, norm2_g,
           w_up, ffn_conv_w, ffn_conv_b, w_down):
    nb, s, _ = x_prompt.shape
    db = x_sample.shape[0]
    depth = w_in.shape[0]
    n_pool = cache_k.shape[1]

    xp = x_prompt.reshape(nb * s, D_MODEL)
    xs = x_sample.reshape(db, D_MODEL)
    logf_flat = cache_logf.reshape(depth, n_pool, PAGE_ROW)

    w_in_b_all = jnp.pad(w_in, ((0, 0), (0, 0), (0, D_PROJ + LANES - w_in.shape[2]))).astype(BF16)
    n_fox_steps = nb * (s // FOX_TQ)

    outs = {name: [] for name in ("fp", "pp", "cp", "ffp", "fs", "ps", "cs", "ffs")}
    kv_prompt = (jnp.zeros((depth, nb * s, ATT_WIDTH), F32), jnp.ones((depth, nb * s, ATT_WIDTH), F32))
    kv_sample = (jnp.zeros((depth, db, ATT_WIDTH), F32), jnp.ones((depth, db, ATT_WIDTH), F32))
    for l in range(depth):
        bf_pad = jnp.pad(b_f[l], (0, LANES - N_HEADS)).reshape(1, LANES)
        pool_w_b = pool_w[l].astype(BF16)
        g1 = norm1_g[l].reshape(1, D_MODEL)
        g2 = norm2_g[l].reshape(1, D_MODEL)
        qg = q_norm_g[l].reshape(1, HEAD_DIM)
        kg = k_norm_g[l].reshape(1, HEAD_DIM)
        ps_l = pool_scale[l].reshape(1, POOL_WIDTH)
        g_pool = out_norm_g[l, :POOL_WIDTH].reshape(1, POOL_WIDTH)
        g_conv = out_norm_g[l, POOL_WIDTH:POOL_WIDTH + CONV_WIDTH].reshape(1, CONV_WIDTH)
        g_att = out_norm_g[l, POOL_WIDTH + CONV_WIDTH:].reshape(1, ATT_WIDTH)
        cb = ffn_conv_b[l].reshape(1, D_FF)

        w_in_b = w_in_b_all
        proj, qb, kp_stack, kb, vp_stack, vb, logf, logf_pad = _inproj(
            l, xp, g1, w_in_b, bf_pad, qg, kg, tm=1024, kv_stack=kv_prompt)
        kv_prompt = (kp_stack, vp_stack)
        ccol, crow = _cumsum(logf_pad, nb, s)
        mix_a, pstate, cstate = _mixers(proj, pool_w_b, ps_l, conv_w[l], g_pool, g_conv, nb, s)
        own_weights = [_CastJob(w_o, l, D_MODEL // n_fox_steps, D_MODEL),
                       _CastJob(w_up, l, D_MODEL // n_fox_steps, 2 * D_FF),
                       _CastJob(w_down, l, D_FF // n_fox_steps, D_MODEL)]
        mix_b, w_o_b, w_up_b, w_down_b = _fox_prompt(qb, kb, vb, ccol, crow, g_att, nb, s, cast_jobs=own_weights,
                                                     tq=FOX_TQ, tk=2 * FOX_TQ)
        outs["fp"].append(logf.reshape(nb, s, N_HEADS))
        outs["pp"].append(pstate)
        outs["cp"].append(cstate)

        proj_s, qb_s, ks_stack, _, vs_stack, _, logf_s, _ = _inproj(
            l, xs, g1, w_in_b, bf_pad, qg, kg, tm=db, kv_stack=kv_sample)
        kv_sample = (ks_stack, vs_stack)
        k32_s, v32_s = ks_stack[l], vs_stack[l]
        lfnew_rep = jnp.tile(logf_s, (1, PAGE_SIZE)).reshape(db, 1, PAGE_ROW)
        bias = _sample_bias(page_table, logf_flat[l], lfnew_rep)
        y_att = _paged_attention(l, page_table, qb_s.reshape(db, N_HEADS, HEAD_DIM), bias,
                                 k32_s.reshape(db, N_HEADS, HEAD_DIM), v32_s.reshape(db, N_HEADS, HEAD_DIM),
                                 cache_k, cache_v)
        mix_a_s, mix_b_s, pstate_t, cstate_t = _sample_mixers(
            proj_s, jnp.swapaxes(state_pool[l], 0, 1), jnp.swapaxes(state_conv[l], 0, 1),
            y_att.reshape(db, ATT_WIDTH), pool_w_b, ps_l, conv_w[l], g_pool, g_conv, g_att)

        x_mid, xs_mid = _outproj(xp, xs, mix_a, mix_b, mix_a_s, mix_b_s, w_o_b)
        xp, fstate, xs, fstate_t = _ffn(x_mid, xs_mid, g2, w_up_b, ffn_conv_w[l], cb, w_down_b,
                                        jnp.swapaxes(state_ffn[l], 0, 1), tm=PROMPT_FFN_TM, seq_len=s)
        outs["ffp"].append(fstate)
        outs["fs"].append(logf_s.reshape(db, 1, N_HEADS))
        outs["ps"].append(jnp.swapaxes(pstate_t, 0, 1))
        outs["cs"].append(jnp.swapaxes(cstate_t, 0, 1))
        outs["ffs"].append(jnp.swapaxes(fstate_t, 0, 1))

    st = lambda name: jnp.stack(outs[name])
    return (xp.reshape(nb, s, D_MODEL), xs.reshape(db, 1, D_MODEL),
            kv_prompt[0].reshape(depth, nb, s, N_HEADS, HEAD_DIM), kv_prompt[1].reshape(depth, nb, s, N_HEADS, HEAD_DIM),
            st("fp"), st("pp"), st("cp"), st("ffp"),
            kv_sample[0].reshape(depth, db, 1, N_HEADS, HEAD_DIM), kv_sample[1].reshape(depth, db, 1, N_HEADS, HEAD_DIM),
            st("fs"), st("ps"), st("cs"), st("ffs"))
```
